```python
import math
import jax, jax.numpy as jnp
from jax import lax
import numpy as np

D_MODEL = 1024
BATCH = 8
SEQ = 4096
DEPTH = 1
DEC_BATCH = 16
DEC_SEQ = 16
PAST_LEN = 1024

CHUNK = 64
N_HEADS = 8
HEAD_DIM = 64
V_DIM = 2 * HEAD_DIM
QK_WIDTH = N_HEADS * 2 * HEAD_DIM
ATTN_WIDTH = N_HEADS * V_DIM
ATTN_SCALE = HEAD_DIM ** -0.5
Q_BLOCK = 128
C_CONV = D_MODEL
CONV_K = 31
SPLITS = [QK_WIDTH, 2 * QK_WIDTH, 2 * QK_WIDTH + ATTN_WIDTH, 2 * QK_WIDTH + ATTN_WIDTH + 2 * C_CONV]
W_IN_COLS = SPLITS[-1] + 2 * D_MODEL
PEER_HEADS = 8
N_KEYS = 128
N_EXPERTS = N_KEYS * N_KEYS
PEER_TOPK = 16
D_KEY = 256
D_KEY_HALF = D_KEY // 2
PEER_BLOCK = 128

EPS = 1e-6
NEG_INF = -1e30

kernel_name = "chunk_causal_diffattn_conformer_peer"


def rms_norm(x, g):
    xf = x.astype(jnp.float32)
    y = xf * lax.rsqrt(jnp.mean(xf * xf, axis=-1, keepdims=True) + EPS)
    return (y * g.astype(jnp.float32)).astype(x.dtype)


def layer_norm(x, g, b):
    xf = x.astype(jnp.float32)
    mu = jnp.mean(xf, axis=-1, keepdims=True)
    xc = xf - mu
    y = xc * lax.rsqrt(jnp.mean(xc * xc, axis=-1, keepdims=True) + EPS)
    return (y * g.astype(jnp.float32) + b.astype(jnp.float32)).astype(x.dtype)


def diff_attend(q, k, v, lam, mask):
    s = jnp.einsum('bqhmd,bkhmd->bhmqk', q, k).astype(jnp.float32) * ATTN_SCALE
    if mask is not None:
        s = jnp.where(mask, s, NEG_INF)
    p = jax.nn.softmax(s, axis=-1)
    a = p[:, :, 0] - lam * p[:, :, 1]
    return jnp.einsum('bhqk,bkhe->bqhe', a, v.astype(jnp.float32))


def prompt_attention(q, k, v, lam):
    B, T = q.shape[0], q.shape[1]
    nb = T // Q_BLOCK
    qb = q.reshape(B, nb, Q_BLOCK, N_HEADS, 2, HEAD_DIM).transpose(1, 0, 2, 3, 4, 5)
    k_chunk = jnp.arange(T) // CHUNK

    def one(args):
        q_blk, i = args
        q_chunk = (i * Q_BLOCK + jnp.arange(Q_BLOCK)) // CHUNK
        mask = k_chunk[None, :] <= q_chunk[:, None]
        return diff_attend(q_blk, k, v, lam, mask)

    o = lax.map(one, (qb, jnp.arange(nb)))
    return o.transpose(1, 0, 2, 3, 4).reshape(B, T, N_HEADS, V_DIM)


def conv_branch(c, hist, w_dw, b_dw, g_cln, b_cln, w_conv_out):
    a, gate = jnp.split(c, 2, axis=-1)
    u = a * jax.nn.sigmoid(gate)
    full = jnp.concatenate([hist, u], axis=1)
    y = lax.conv_general_dilated(full, w_dw, window_strides=(1,), padding='VALID',
                                 dimension_numbers=('NWC', 'WIO', 'NWC'),
                                 feature_group_count=C_CONV) + b_dw
    y = jax.nn.silu(layer_norm(y, g_cln, b_cln))
    return y @ w_conv_out, full[:, -(CONV_K - 1):]


def peer(xn, w_pq, sub_keys, expert_u, expert_v):
    B, T, D = xn.shape
    n = B * T
    pad = (-n) % PEER_BLOCK
    xb = jnp.pad(xn.reshape(n, D), ((0, pad), (0, 0))).reshape(-1, PEER_BLOCK, D)

    def one(xt):
        q = (xt @ w_pq).reshape(PEER_BLOCK, PEER_HEADS, 2, D_KEY_HALF)
        s = jnp.einsum('thpd,hpkd->thpk', q, sub_keys).astype(jnp.float32)
        ts, ti = lax.top_k(s, PEER_TOPK)
        cand_s = (ts[:, :, 0, :, None] + ts[:, :, 1, None, :]).reshape(PEER_BLOCK, PEER_HEADS, PEER_TOPK * PEER_TOPK)
        cand_i = (ti[:, :, 0, :, None] * N_KEYS + ti[:, :, 1, None, :]).reshape(PEER_BLOCK, PEER_HEADS, PEER_TOPK * PEER_TOPK)
        best_s, best_j = lax.top_k(cand_s, PEER_TOPK)
        idx = jnp.take_along_axis(cand_i, best_j, axis=-1)
        g = jax.nn.softmax(best_s, axis=-1)
        u = expert_u[idx]
        act = jax.nn.gelu(jnp.einsum('thkd,td->thk', u, xt).astype(jnp.float32))
        w = (g * act).astype(xt.dtype)
        return jnp.einsum('thk,thkd->td', w, expert_v[idx])

    out = lax.map(one, xb).reshape(-1, D)[:n]
    return out.reshape(B, T, D)


def layer_forward(x, cache_k, cache_v, cache_conv, lambda_init,
                  g_mix, w_in, b_gate, g_qn, g_kn, lam_q1, lam_k1, lam_q2, lam_k2, g_sub,
                  w_attn_out, w_dw, b_dw, g_cln, b_cln, w_conv_out, w_out,
                  g_ffn, w_pq, sub_keys, expert_u, expert_v):
    B, T, _ = x.shape
    xn = rms_norm(x, g_mix)
    q, k, v, c, gt = jnp.split(xn @ w_in, SPLITS, axis=-1)
    q = rms_norm(q.reshape(B, T, N_HEADS, 2, HEAD_DIM), g_qn)
    k = rms_norm(k.reshape(B, T, N_HEADS, 2, HEAD_DIM), g_kn)
    v = v.reshape(B, T, N_HEADS, V_DIM)
    lam = (jnp.exp(jnp.sum(lam_q1.astype(jnp.float32) * lam_k1.astype(jnp.float32)))
           - jnp.exp(jnp.sum(lam_q2.astype(jnp.float32) * lam_k2.astype(jnp.float32)))
           + lambda_init)
    if cache_k is None:
        o = prompt_attention(q, k, v, lam)
        hist = jnp.zeros((B, CONV_K - 1, C_CONV), x.dtype)
    else:
        o = diff_attend(q, jnp.concatenate([cache_k, k], axis=1),
                        jnp.concatenate([cache_v, v], axis=1), lam, None)
        hist = cache_conv
    o = (rms_norm(o, g_sub) * (1.0 - lambda_init)).astype(x.dtype).reshape(B, T, ATTN_WIDTH)
    attn_out = o @ w_attn_out
    conv_out, conv_state = conv_branch(c, hist, w_dw, b_dw, g_cln, b_cln, w_conv_out)
    ga, gc = jnp.split(jax.nn.sigmoid((gt + b_gate).astype(jnp.float32)).astype(x.dtype), 2, axis=-1)
    h = x + (ga * attn_out + gc * conv_out) @ w_out
    y = h + peer(rms_norm(h, g_ffn), w_pq, sub_keys, expert_u, expert_v)
    return y, k, v, conv_state


def setup_inputs(seed: int = 0) -> dict:
    key = jax.random.key(seed)
    ks = jax.random.split(key, 32)
    f32 = jnp.float32

    def nrm(k, shape, scale):
        return jax.random.normal(k, shape, f32) * scale

    L = DEPTH
    return {
        "x_prompt": nrm(ks[0], (BATCH, SEQ, D_MODEL), 1.0),
        "x_sample": nrm(ks[1], (DEC_BATCH, DEC_SEQ, D_MODEL), 1.0),
        "cache_k": nrm(ks[2], (L, DEC_BATCH, PAST_LEN, N_HEADS, 2, HEAD_DIM), 1.0),
        "cache_v": nrm(ks[3], (L, DEC_BATCH, PAST_LEN, N_HEADS, V_DIM), 1.0),
        "cache_conv": nrm(ks[4], (L, DEC_BATCH, CONV_K - 1, C_CONV), 1.0),
        "g_mix": 1.0 + nrm(ks[5], (L, D_MODEL), 0.02),
        "w_in": nrm(ks[6], (L, D_MODEL, W_IN_COLS), D_MODEL ** -0.5),
        "b_gate": nrm(ks[7], (L, 2 * D_MODEL), 0.01),
        "g_qn": 1.0 + nrm(ks[8], (L, HEAD_DIM), 0.02),
        "g_kn": 1.0 + nrm(ks[9], (L, HEAD_DIM), 0.02),
        "lam_q1": nrm(ks[10], (L, HEAD_DIM), 0.1),
        "lam_k1": nrm(ks[11], (L, HEAD_DIM), 0.1),
        "lam_q2": nrm(ks[12], (L, HEAD_DIM), 0.1),
        "lam_k2": nrm(ks[13], (L, HEAD_DIM), 0.1),
        "g_sub": 1.0 + nrm(ks[14], (L, V_DIM), 0.02),
        "w_attn_out": nrm(ks[15], (L, ATTN_WIDTH, D_MODEL), ATTN_WIDTH ** -0.5),
        "w_dw": nrm(ks[16], (L, CONV_K, 1, C_CONV), CONV_K ** -0.5),
        "b_dw": nrm(ks[17], (L, C_CONV), 0.01),
        "g_cln": 1.0 + nrm(ks[18], (L, C_CONV), 0.02),
        "b_cln": nrm(ks[19], (L, C_CONV), 0.01),
        "w_conv_out": nrm(ks[20], (L, C_CONV, D_MODEL), C_CONV ** -0.5),
        "w_out": nrm(ks[21], (L, D_MODEL, D_MODEL), D_MODEL ** -0.5),
        "g_ffn": 1.0 + nrm(ks[22], (L, D_MODEL), 0.02),
        "w_pq": nrm(ks[23], (L, D_MODEL, PEER_HEADS * D_KEY), D_MODEL ** -0.5),
        "sub_keys": nrm(ks[24], (L, PEER_HEADS, 2, N_KEYS, D_KEY_HALF), D_KEY_HALF ** -0.5),
        "expert_u": nrm(ks[25], (L, N_EXPERTS, D_MODEL), D_MODEL ** -0.5),
        "expert_v": nrm(ks[26], (L, N_EXPERTS, D_MODEL), PEER_HEADS ** -0.5),
    }


def reference(x_prompt, x_sample, cache_k, cache_v, cache_conv,
              g_mix, w_in, b_gate, g_qn, g_kn, lam_q1, lam_k1, lam_q2, lam_k2, g_sub,
              w_attn_out, w_dw, b_dw, g_cln, b_cln, w_conv_out, w_out,
              g_ffn, w_pq, sub_keys, expert_u, expert_v):
    yp, ys = x_prompt, x_sample
    kp_l, vp_l, cp_l, ks_l, vs_l, cs_l = [], [], [], [], [], []
    for l in range(DEPTH):
        lambda_init = 0.8 - 0.6 * math.exp(-0.3 * l)
        w = (g_mix[l], w_in[l], b_gate[l], g_qn[l], g_kn[l], lam_q1[l], lam_k1[l],
             lam_q2[l], lam_k2[l], g_sub[l], w_attn_out[l], w_dw[l], b_dw[l], g_cln[l],
             b_cln[l], w_conv_out[l], w_out[l], g_ffn[l], w_pq[l], sub_keys[l],
             expert_u[l], expert_v[l])
        yp, kp, vp, cp = layer_forward(yp, None, None, None, lambda_init, *w)
        ys, kn, vn, cn = layer_forward(ys, cache_k[l], cache_v[l], cache_conv[l], lambda_init, *w)
        kp_l.append(kp); vp_l.append(vp); cp_l.append(cp)
        ks_l.append(kn); vs_l.append(vn); cs_l.append(cn)
    k_prompt = jnp.stack(kp_l)
    v_prompt = jnp.stack(vp_l)
    conv_prompt = jnp.stack(cp_l)
    k_sample = jnp.stack(ks_l)
    v_sample = jnp.stack(vs_l)
    conv_sample = jnp.stack(cs_l)
    return (yp, ys, k_prompt, v_prompt, conv_prompt, k_sample, v_sample, conv_sample)
```

```python
import functools
import math

import jax
import jax.numpy as jnp
from jax import lax
from jax.experimental import pallas as pl
from jax.experimental.pallas import tpu as pltpu

D_MODEL = 1024
CHUNK = 64
N_HEADS = 8
HEAD_DIM = 64
V_DIM = 2 * HEAD_DIM
ATTN_SCALE = HEAD_DIM ** -0.5
CONV_K = 31
PEER_HEADS = 8
N_KEYS = 128
PEER_TOPK = 16
EPS = 1e-6
NEG_INF = -1e30

LANES = 128
HIST_ROWS = 32
VMEM_LIMIT = 48 * 1024 * 1024

_BF16 = jnp.bfloat16
_F32 = jnp.float32


def _resident(shape):
    nd = len(shape)
    return pl.BlockSpec(shape, lambda *_: (0,) * nd, pipeline_mode=pl.Buffered(1))


def _params(*sem):
    return pltpu.CompilerParams(dimension_semantics=sem, vmem_limit_bytes=VMEM_LIMIT)


def _inproj_kernel(x_ref, gmix_ref, w_ref, bg_ref, gq_ref, gk_ref,
                   qs_ref, kf_ref, kb_ref, vf_ref, vb_ref, u_ref, gt_ref):
    x = x_ref[...]
    ms = jnp.mean(x * x, axis=-1, keepdims=True)
    xn = (x * lax.rsqrt(ms + EPS) * gmix_ref[...]).astype(_BF16)

    def proj(c):
        return jnp.dot(xn, w_ref[:, c * D_MODEL:(c + 1) * D_MODEL], preferred_element_type=_F32)

    lo = lax.broadcasted_iota(jnp.int32, (1, LANES), 1) < HEAD_DIM

    def head_norm(zh, g):
        sq = zh * zh
        ss_lo = jnp.sum(jnp.where(lo, sq, 0.0), axis=-1, keepdims=True)
        ss_hi = jnp.sum(jnp.where(lo, 0.0, sq), axis=-1, keepdims=True)
        r = jnp.where(lo, lax.rsqrt(ss_lo / HEAD_DIM + EPS), lax.rsqrt(ss_hi / HEAD_DIM + EPS))
        return zh * r * g

    zq = proj(0)
    for h in range(N_HEADS):
        sl = slice(h * LANES, (h + 1) * LANES)
        qn = head_norm(zq[:, sl], gq_ref[...]) * ATTN_SCALE
        qs_ref[0, :, sl] = jnp.where(lo, qn, 0.0).astype(_BF16)
        qs_ref[1, :, sl] = jnp.where(lo, 0.0, qn).astype(_BF16)
    zk = proj(1)
    for h in range(N_HEADS):
        sl = slice(h * LANES, (h + 1) * LANES)
        kn = head_norm(zk[:, sl], gk_ref[...])
        kf_ref[:, sl] = kn
        kb_ref[:, sl] = kn.astype(_BF16)
    zv = proj(2)
    vf_ref[...] = zv
    vb_ref[...] = zv.astype(_BF16)
    u_ref[...] = proj(3) * jax.nn.sigmoid(proj(4))
    gt_ref[:, :D_MODEL] = jax.nn.sigmoid(proj(5) + bg_ref[:, :D_MODEL])
    gt_ref[:, D_MODEL:] = jax.nn.sigmoid(proj(6) + bg_ref[:, D_MODEL:])


def _inproj(x, g_mix, w_in_bf, b_gate, gq, gk, tm):
    n = x.shape[0]
    row = lambda w: pl.BlockSpec((tm, w), lambda i: (i, 0))
    return pl.pallas_call(
        _inproj_kernel,
        grid=(n // tm,),
        in_specs=[row(D_MODEL), _resident((1, D_MODEL)), _resident(w_in_bf.shape),
                  _resident((1, 2 * D_MODEL)), _resident((1, LANES)), _resident((1, LANES))],
        out_specs=[pl.BlockSpec((2, tm, D_MODEL), lambda i: (0, i, 0)),
                   row(D_MODEL), row(D_MODEL), row(D_MODEL), row(D_MODEL), row(D_MODEL),
                   row(2 * D_MODEL)],
        out_shape=[jax.ShapeDtypeStruct((2, n, D_MODEL), _BF16),
                   jax.ShapeDtypeStruct((n, D_MODEL), _F32),
                   jax.ShapeDtypeStruct((n, D_MODEL), _BF16),
                   jax.ShapeDtypeStruct((n, D_MODEL), _F32),
                   jax.ShapeDtypeStruct((n, D_MODEL), _BF16),
                   jax.ShapeDtypeStruct((n, D_MODEL), _F32),
                   jax.ShapeDtypeStruct((n, 2 * D_MODEL), _F32)],
        compiler_params=_params("parallel"),
        name="inproj",
    )(x, g_mix, w_in_bf, b_gate, gq, gk)


def _lambda(lam_ref, lambda_init):
    lam = lam_ref[...]
    a = jnp.sum(lam[0:1] * lam[1:2], axis=-1, keepdims=True)
    b = jnp.sum(lam[2:3] * lam[3:4], axis=-1, keepdims=True)
    return jnp.exp(a) - jnp.exp(b) + lambda_init


def _attn_finish(acc, l, tq, lam, gsub_ref, lambda_init):
    o = acc / l
    o = o[:tq] - lam * o[tq:]
    o = o * lax.rsqrt(jnp.mean(o * o, axis=-1, keepdims=True) + EPS) * gsub_ref[...]
    return (o * (1.0 - lambda_init)).astype(_BF16)


def _qk(q, k):
    return lax.dot_general(q, k, (((1,), (1,)), ((), ())), preferred_element_type=_F32)


def _prompt_attn_kernel(q_ref, k_ref, v_ref, lam_ref, gsub_ref, o_ref, *, tq, lambda_init):
    qi = pl.program_id(2)
    q = jnp.concatenate([q_ref[0], q_ref[1]], axis=0)

    def step(j, carry, masked):
        m, l, acc = carry
        off = pl.multiple_of(j * tq, tq)
        s = _qk(q, k_ref[pl.ds(off, tq), :])
        if masked:
            qc = (lax.broadcasted_iota(jnp.int32, s.shape, 0) % tq) // CHUNK
            kc = lax.broadcasted_iota(jnp.int32, s.shape, 1) // CHUNK
            s = jnp.where(kc <= qc, s, NEG_INF)
        m_new = jnp.maximum(m, jnp.max(s, axis=-1, keepdims=True))
        alpha = jnp.exp(m - m_new)
        p = jnp.exp(s - m_new)
        l = alpha * l + jnp.sum(p, axis=-1, keepdims=True)
        acc = alpha * acc + jnp.dot(p.astype(_BF16), v_ref[pl.ds(off, tq), :],
                                    preferred_element_type=_F32)
        return m_new, l, acc

    init = (jnp.full((2 * tq, 1), NEG_INF, _F32), jnp.zeros((2 * tq, 1), _F32),
            jnp.zeros((2 * tq, V_DIM), _F32))
    carry = lax.fori_loop(0, qi, lambda j, c: step(j, c, False), init)
    _, l, acc = step(qi, carry, True)
    o_ref[...] = _attn_finish(acc, l, tq, _lambda(lam_ref, lambda_init), gsub_ref, lambda_init)


def _prompt_attention(qs, kb, vb, lam4, gsub, batch, seq, tq, lambda_init):
    n = batch * seq
    nq = seq // tq
    return pl.pallas_call(
        functools.partial(_prompt_attn_kernel, tq=tq, lambda_init=lambda_init),
        grid=(batch, N_HEADS, nq),
        in_specs=[pl.BlockSpec((2, tq, LANES), lambda b, h, i: (0, b * nq + i, h)),
                  pl.BlockSpec((seq, LANES), lambda b, h, i: (b, h)),
                  pl.BlockSpec((seq, LANES), lambda b, h, i: (b, h)),
                  _resident((4, HEAD_DIM)), _resident((1, V_DIM))],
        out_specs=pl.BlockSpec((tq, LANES), lambda b, h, i: (b * nq + i, h)),
        out_shape=jax.ShapeDtypeStruct((n, D_MODEL), _BF16),
        compiler_params=_params("parallel", "parallel", "arbitrary"),
        name="prompt_attn",
    )(qs, kb, vb, lam4, gsub)


def _decode_attn_kernel(q_ref, kn_ref, vn_ref, kc_ref, vc_ref, lam_ref, gsub_ref, o_ref, *, tq, lambda_init):
    q = jnp.concatenate([q_ref[0], q_ref[1]], axis=0)
    s_c = _qk(q, kc_ref[0].astype(_BF16))
    s_n = _qk(q, kn_ref[...])
    m = jnp.maximum(jnp.max(s_c, axis=-1, keepdims=True), jnp.max(s_n, axis=-1, keepdims=True))
    p_c = jnp.exp(s_c - m)
    p_n = jnp.exp(s_n - m)
    l = jnp.sum(p_c, axis=-1, keepdims=True) + jnp.sum(p_n, axis=-1, keepdims=True)
    acc = (jnp.dot(p_c.astype(_BF16), vc_ref[0].astype(_BF16), preferred_element_type=_F32)
           + jnp.dot(p_n.astype(_BF16), vn_ref[...], preferred_element_type=_F32))
    o_ref[...] = _attn_finish(acc, l, tq, _lambda(lam_ref, lambda_init), gsub_ref, lambda_init)


def _decode_attention(qs, kb, vb, cache_k, cache_v, lam4, gsub, batch, seq, lambda_init):
    past = cache_k.shape[1]
    return pl.pallas_call(
        functools.partial(_decode_attn_kernel, tq=seq, lambda_init=lambda_init),
        grid=(batch, N_HEADS),
        in_specs=[pl.BlockSpec((2, seq, LANES), lambda b, h: (0, b, h)),
                  pl.BlockSpec((seq, LANES), lambda b, h: (b, h)),
                  pl.BlockSpec((seq, LANES), lambda b, h: (b, h)),
                  pl.BlockSpec((1, past, LANES), lambda b, h: (b, 0, h)),
                  pl.BlockSpec((1, past, LANES), lambda b, h: (b, 0, h)),
                  _resident((4, HEAD_DIM)), _resident((1, V_DIM))],
        out_specs=pl.BlockSpec((seq, LANES), lambda b, h: (b, h)),
        out_shape=jax.ShapeDtypeStruct((batch * seq, D_MODEL), _BF16),
        compiler_params=_params("parallel", "parallel"),
        name="decode_attn",
    )(qs, kb, vb, cache_k, cache_v, lam4, gsub)


CONV_ROWS = 32


def _conv_kernel(u_ref, hist_ref, w_ref, b_ref, g_ref, beta_ref, y_ref, win_ref, *, tt):
    @pl.when(pl.program_id(1) == 0)
    def _():
        win_ref[0:HIST_ROWS, :] = hist_ref[0]

    win_ref[HIST_ROWS:HIST_ROWS + tt, :] = u_ref[0]
    first = HIST_ROWS - (CONV_K - 1)
    rows = min(CONV_ROWS, tt)
    for r0 in range(0, tt, rows):
        acc = jnp.broadcast_to(b_ref[...], (rows, D_MODEL))
        for j in range(CONV_K):
            acc = acc + w_ref[j:j + 1, :] * win_ref[first + r0 + j:first + r0 + j + rows, :]
        mu = jnp.mean(acc, axis=-1, keepdims=True)
        xc = acc - mu
        y = xc * lax.rsqrt(jnp.mean(xc * xc, axis=-1, keepdims=True) + EPS)
        y = y * g_ref[...] + beta_ref[...]
        y_ref[0, r0:r0 + rows, :] = (y * jax.nn.sigmoid(y)).astype(_BF16)
    win_ref[0:HIST_ROWS, :] = win_ref[tt:tt + HIST_ROWS, :]


def _conv_branch(u, hist, w_dw, b_dw, g_cln, b_cln, tt):
    batch, seq, _ = u.shape
    return pl.pallas_call(
        functools.partial(_conv_kernel, tt=tt),
        grid=(batch, seq // tt),
        in_specs=[pl.BlockSpec((1, tt, D_MODEL), lambda b, i: (b, i, 0)),
                  pl.BlockSpec((1, HIST_ROWS, D_MODEL), lambda b, i: (b, 0, 0)),
                  _resident((HIST_ROWS, D_MODEL)), _resident((1, D_MODEL)),
                  _resident((1, D_MODEL)), _resident((1, D_MODEL))],
        out_specs=pl.BlockSpec((1, tt, D_MODEL), lambda b, i: (b, i, 0)),
        out_shape=jax.ShapeDtypeStruct((batch, seq, D_MODEL), _BF16),
        scratch_shapes=[pltpu.VMEM((HIST_ROWS + max(tt, HIST_ROWS), D_MODEL), _F32)],
        compiler_params=_params("parallel", "arbitrary"),
        name="conv_branch",
    )(u, hist, w_dw, b_dw, g_cln, b_cln)


def _merge_kernel(o_ref, c_ref, gt_ref, x_ref, wa_ref, wc_ref, wo_ref, gffn_ref, h_ref, hn_ref):
    attn = jnp.dot(o_ref[...], wa_ref[...], preferred_element_type=_F32)
    conv = jnp.dot(c_ref[...], wc_ref[...], preferred_element_type=_F32)
    mix = gt_ref[:, :D_MODEL] * attn + gt_ref[:, D_MODEL:] * conv
    h = x_ref[...] + jnp.dot(mix.astype(_BF16), wo_ref[...], preferred_element_type=_F32)
    h_ref[...] = h
    hn = h * lax.rsqrt(jnp.mean(h * h, axis=-1, keepdims=True) + EPS) * gffn_ref[...]
    hn_ref[...] = hn.astype(_BF16).astype(_F32)


def _merge(o, c, gt, x, wa, wc, wo, g_ffn, tm):
    n = x.shape[0]
    row = lambda w: pl.BlockSpec((tm, w), lambda i: (i, 0))
    sq = (D_MODEL, D_MODEL)
    return pl.pallas_call(
        _merge_kernel,
        grid=(n // tm,),
        in_specs=[row(D_MODEL), row(D_MODEL), row(2 * D_MODEL), row(D_MODEL),
                  _resident(sq), _resident(sq), _resident(sq), _resident((1, D_MODEL))],
        out_specs=[row(D_MODEL), row(D_MODEL)],
        out_shape=[jax.ShapeDtypeStruct((n, D_MODEL), _F32), jax.ShapeDtypeStruct((n, D_MODEL), _F32)],
        compiler_params=_params("parallel"),
        name="merge",
    )(o, c, gt, x, wa, wc, wo, g_ffn)


_KEY_SENTINEL = N_KEYS * N_KEYS


def _topk_rows(s, keys, k):
    vals, sels = [], []
    for r in range(k):
        m = jnp.max(s, axis=0, keepdims=True)
        sel = jnp.min(jnp.where(s == m, keys, _KEY_SENTINEL), axis=0, keepdims=True)
        vals.append(m)
        sels.append(sel)
        if r + 1 < k:
            s = jnp.where(keys == sel, -jnp.inf, s)
    return jnp.concatenate(vals, axis=0), jnp.concatenate(sels, axis=0)


def _route_kernel(hn_ref, wpq_ref, sk_ref, idx_ref, g_ref, *, tm):
    q = jnp.dot(hn_ref[...].astype(_BF16), wpq_ref[...], preferred_element_type=_F32).astype(_BF16)
    key_iota = lax.broadcasted_iota(jnp.int32, (N_KEYS, LANES), 0)
    for lt in range(tm // LANES):
        rows = slice(lt * LANES, (lt + 1) * LANES)
        idx_parts, g_parts = [], []
        for h in range(PEER_HEADS):
            top = []
            for p in range(2):
                hp = 2 * h + p
                st = _qk(sk_ref[hp], q[rows, hp * LANES:(hp + 1) * LANES])
                top.append(_topk_rows(st, key_iota, PEER_TOPK))
            (ts0, ti0), (ts1, ti1) = top
            cand_s = jnp.concatenate([ts0[i:i + 1] + ts1 for i in range(PEER_TOPK)], axis=0)
            cand_i = jnp.concatenate([ti0[i:i + 1] * N_KEYS + ti1 for i in range(PEER_TOPK)], axis=0)
            best_s, best_i = _topk_rows(cand_s, cand_i, PEER_TOPK)
            e = jnp.exp(best_s - best_s[0:1])
            g_parts.append(e / jnp.sum(e, axis=0, keepdims=True))
            idx_parts.append(best_i)
        idx_ref[rows, :] = jnp.concatenate(idx_parts, axis=0).T
        g_ref[rows, :] = jnp.concatenate(g_parts, axis=0).T


def _route(hn, w_pq_bf, sk_bf, tm):
    n = hn.shape[0]
    width = PEER_HEADS * PEER_TOPK
    return pl.pallas_call(
        functools.partial(_route_kernel, tm=tm),
        grid=(n // tm,),
        in_specs=[pl.BlockSpec((tm, D_MODEL), lambda i: (i, 0)),
                  _resident(w_pq_bf.shape), _resident(sk_bf.shape)],
        out_specs=[pl.BlockSpec((tm, width), lambda i: (i, 0)), pl.BlockSpec((tm, width), lambda i: (i, 0))],
        out_shape=[jax.ShapeDtypeStruct((n, width), jnp.int32), jax.ShapeDtypeStruct((n, width), _F32)],
        compiler_params=_params("parallel"),
        name="peer_route",
    )(hn, w_pq_bf, sk_bf)


PEER_WIDTH = PEER_HEADS * PEER_TOPK
HALF = D_MODEL // 2
GATHER_SLOTS = 4
GATHER_TOKENS = 64


def _pack_experts(expert_u, expert_v):
    def words(t):
        b = lax.bitcast_convert_type(t.astype(_BF16), jnp.uint16).astype(jnp.uint32)
        return b[:, :HALF] | (b[:, HALF:] << 16)
    return jnp.concatenate([words(expert_u), words(expert_v)], axis=1)


def _unpack(words):
    lo = lax.bitcast_convert_type(words << 16, _F32)
    hi = lax.bitcast_convert_type(words & jnp.uint32(0xFFFF0000), _F32)
    return lo, hi


def _retrieve_kernel(idx_hbm, tab_hbm, g_ref, hn_ref, h_ref, y_ref, idx_smem, buf, idx_sem, row_sems):
    tb = GATHER_TOKENS
    blk = pl.program_id(0)
    idx_copy = pltpu.make_async_copy(idx_hbm.at[pl.ds(blk * tb * PEER_WIDTH, tb * PEER_WIDTH)], idx_smem, idx_sem)
    idx_copy.start()
    idx_copy.wait()

    def issue(t, slot):
        for j in range(PEER_WIDTH):
            e = idx_smem[t * PEER_WIDTH + j]
            pltpu.make_async_copy(tab_hbm.at[pl.ds(e, 1), :], buf.at[slot, pl.ds(j, 1), :],
                                  row_sems.at[slot]).start()

    def wait(slot):
        pltpu.make_async_copy(tab_hbm.at[pl.ds(0, PEER_WIDTH), :], buf.at[slot], row_sems.at[slot]).wait()

    for s in range(GATHER_SLOTS - 1):
        issue(s, s)

    eye = (lax.broadcasted_iota(jnp.int32, (PEER_WIDTH, PEER_WIDTH), 0)
           == lax.broadcasted_iota(jnp.int32, (PEER_WIDTH, PEER_WIDTH), 1))

    def token(t, _):
        slot = t % GATHER_SLOTS
        ahead = t + GATHER_SLOTS - 1

        @pl.when(ahead < tb)
        def _():
            issue(ahead, ahead % GATHER_SLOTS)

        wait(slot)
        w = buf[slot]
        x = hn_ref[pl.ds(t, 1), :]
        u_lo, u_hi = _unpack(w[:, :HALF])
        d = jnp.sum(u_lo * x[:, :HALF] + u_hi * x[:, HALF:], axis=-1, keepdims=True)
        g_row = g_ref[pl.ds(t, 1), :]
        g_col = jnp.sum(jnp.where(eye, g_row, 0.0), axis=-1, keepdims=True)
        wgt = g_col * jax.nn.gelu(d)
        v_lo, v_hi = _unpack(w[:, HALF:])
        o_lo = jnp.sum(wgt * v_lo, axis=0, keepdims=True)
        o_hi = jnp.sum(wgt * v_hi, axis=0, keepdims=True)
        y_ref[pl.ds(t, 1), :HALF] = h_ref[pl.ds(t, 1), :HALF] + o_lo
        y_ref[pl.ds(t, 1), HALF:] = h_ref[pl.ds(t, 1), HALF:] + o_hi
        return 0

    lax.fori_loop(0, tb, token, 0)


def _retrieve(idx, table, g, hn, h):
    n = h.shape[0]
    tb = GATHER_TOKENS
    row = lambda w: pl.BlockSpec((tb, w), lambda i: (i, 0))
    return pl.pallas_call(
        _retrieve_kernel,
        grid=(n // tb,),
        in_specs=[pl.BlockSpec(memory_space=pl.ANY), pl.BlockSpec(memory_space=pl.ANY),
                  row(PEER_WIDTH), row(D_MODEL), row(D_MODEL)],
        out_specs=row(D_MODEL),
        out_shape=jax.ShapeDtypeStruct((n, D_MODEL), _F32),
        scratch_shapes=[pltpu.SMEM((tb * PEER_WIDTH,), jnp.int32),
                        pltpu.VMEM((GATHER_SLOTS, PEER_WIDTH, D_MODEL), jnp.uint32),
                        pltpu.SemaphoreType.DMA,
                        pltpu.SemaphoreType.DMA((GATHER_SLOTS,))],
        compiler_params=_params("arbitrary"),
        name="peer_retrieve",
    )(idx.reshape(-1), table, g, hn, h)


def _row_tile(n):
    return min(256, n)


def _layer(x, cache_k, cache_v, hist, lambda_init, wts):
    batch, seq, _ = x.shape
    n = batch * seq
    xf = x.reshape(n, D_MODEL)
    tm = _row_tile(n)
    qs, kf, kb, vf, vb, u, gt = _inproj(xf, wts["g_mix"], wts["w_in"], wts["b_gate"], wts["gq"], wts["gk"], tm)
    if cache_k is None:
        o = _prompt_attention(qs, kb, vb, wts["lam4"], wts["g_sub"], batch, seq, min(256, seq), lambda_init)
    else:
        o = _decode_attention(qs, kb, vb, cache_k, cache_v, wts["lam4"], wts["g_sub"], batch, seq, lambda_init)
    u3 = u.reshape(batch, seq, D_MODEL)
    c = _conv_branch(u3, hist, wts["w_dw"], wts["b_dw"], wts["g_cln"], wts["b_cln"], min(256, seq))
    h, hn = _merge(o, c.reshape(n, D_MODEL), gt, xf, wts["w_attn_out"], wts["w_conv_out"], wts["w_out"],
                   wts["g_ffn"], tm)
    idx, g = _route(hn, wts["w_pq"], wts["sub_keys"], tm)
    y = _retrieve(idx, wts["table"], g, hn, h)
    keep = CONV_K - 1
    if seq >= keep:
        conv_state = u3[:, seq - keep:]
    else:
        conv_state = jnp.concatenate([hist[:, HIST_ROWS - keep + seq:], u3], axis=1)
    return (y.reshape(batch, seq, D_MODEL), kf.reshape(batch, seq, N_HEADS, 2, HEAD_DIM),
            vf.reshape(batch, seq, N_HEADS, V_DIM), conv_state)


def _layer_weights(l, g_mix, w_in, b_gate, g_qn, g_kn, lam_q1, lam_k1, lam_q2, lam_k2, g_sub,
                   w_attn_out, w_dw, b_dw, g_cln, b_cln, w_conv_out, w_out, g_ffn, w_pq, sub_keys,
                   expert_u, expert_v):
    row = lambda a: a[l].reshape(1, -1)
    return dict(
        g_mix=row(g_mix), w_in=w_in[l].astype(_BF16), b_gate=row(b_gate),
        gq=jnp.tile(g_qn[l], 2).reshape(1, LANES), gk=jnp.tile(g_kn[l], 2).reshape(1, LANES),
        lam4=jnp.stack([lam_q1[l], lam_k1[l], lam_q2[l], lam_k2[l]]), g_sub=row(g_sub),
        w_attn_out=w_attn_out[l].astype(_BF16),
        w_dw=jnp.pad(w_dw[l].reshape(CONV_K, D_MODEL), ((0, HIST_ROWS - CONV_K), (0, 0))),
        b_dw=row(b_dw), g_cln=row(g_cln), b_cln=row(b_cln),
        w_conv_out=w_conv_out[l].astype(_BF16), w_out=w_out[l].astype(_BF16), g_ffn=row(g_ffn),
        w_pq=w_pq[l].astype(_BF16),
        sub_keys=sub_keys[l].reshape(2 * PEER_HEADS, N_KEYS, -1).astype(_BF16),
        table=_pack_experts(expert_u[l], expert_v[l]),
    )


def kernel(x_prompt, x_sample, cache_k, cache_v, cache_conv, g_mix, w_in, b_gate, g_qn, g_kn, lam_q1, lam_k1, lam_q2, lam_k2, g_sub, w_attn_out, w_dw, b_dw, g_cln, b_cln, w_conv_out, w_out, g_ffn, w_pq, sub_keys, expert_u, expert_v):
    depth = w_in.shape[0]
    yp, ys = x_prompt, x_sample
    pad_hist = lambda hst: jnp.pad(hst, ((0, 0), (HIST_ROWS - (CONV_K - 1), 0), (0, 0)))
    zero_hist = jnp.zeros((x_prompt.shape[0], HIST_ROWS, D_MODEL), x_prompt.dtype)
    outs = [[] for _ in range(6)]
    for l in range(depth):
        lambda_init = 0.8 - 0.6 * math.exp(-0.3 * l)
        wts = _layer_weights(l, g_mix, w_in, b_gate, g_qn, g_kn, lam_q1, lam_k1, lam_q2, lam_k2, g_sub,
                             w_attn_out, w_dw, b_dw, g_cln, b_cln, w_conv_out, w_out, g_ffn, w_pq,
                             sub_keys, expert_u, expert_v)
        yp, kp, vp, cp = _layer(yp, None, None, zero_hist, lambda_init, wts)
        db, past = cache_k.shape[1], cache_k.shape[2]
        ys, kn, vn, cn = _layer(ys, cache_k[l].reshape(db, past, D_MODEL), cache_v[l].reshape(db, past, D_MODEL),
                                pad_hist(cache_conv[l]), lambda_init, wts)
        for lst, val in zip(outs, (kp, vp, cp, kn, vn, cn)):
            lst.append(val)
    return (yp, ys) + tuple(jnp.stack(lst) for lst in outs)
```

```python
import functools
import math

import jax
import jax.numpy as jnp
from jax import lax
from jax.experimental import pallas as pl
from jax.experimental.pallas import tpu as pltpu
from jax.experimental.pallas import tpu_sc as plsc

D_MODEL = 1024
CHUNK = 64
N_HEADS = 8
HEAD_DIM = 64
V_DIM = 2 * HEAD_DIM
ATTN_SCALE = HEAD_DIM ** -0.5
CONV_K = 31
PEER_HEADS = 8
N_KEYS = 128
PEER_TOPK = 16
EPS = 1e-6
NEG_INF = -1e30

LANES = 128
HIST_ROWS = 32
VMEM_LIMIT = 48 * 1024 * 1024

_BF16 = jnp.bfloat16
_F32 = jnp.float32


def _resident(shape):
    nd = len(shape)
    return pl.BlockSpec(shape, lambda *_: (0,) * nd, pipeline_mode=pl.Buffered(1))


def _params(*sem):
    return pltpu.CompilerParams(dimension_semantics=sem, vmem_limit_bytes=VMEM_LIMIT)


def _inproj_kernel(x_ref, gmix_ref, w_ref, bg_ref, gq_ref, gk_ref,
                   qs_ref, kf_ref, kb_ref, vf_ref, vb_ref, u_ref, gt_ref):
    x = x_ref[...]
    ms = jnp.mean(x * x, axis=-1, keepdims=True)
    xn = (x * lax.rsqrt(ms + EPS) * gmix_ref[...]).astype(_BF16)

    def proj(c):
        return jnp.dot(xn, w_ref[:, c * D_MODEL:(c + 1) * D_MODEL], preferred_element_type=_F32)

    lo = lax.broadcasted_iota(jnp.int32, (1, LANES), 1) < HEAD_DIM

    def head_norm(zh, g):
        sq = zh * zh
        ss_lo = jnp.sum(jnp.where(lo, sq, 0.0), axis=-1, keepdims=True)
        ss_hi = jnp.sum(jnp.where(lo, 0.0, sq), axis=-1, keepdims=True)
        r = jnp.where(lo, lax.rsqrt(ss_lo / HEAD_DIM + EPS), lax.rsqrt(ss_hi / HEAD_DIM + EPS))
        return zh * r * g

    zq = proj(0)
    for h in range(N_HEADS):
        sl = slice(h * LANES, (h + 1) * LANES)
        qn = head_norm(zq[:, sl], gq_ref[...]) * ATTN_SCALE
        qs_ref[0, :, sl] = jnp.where(lo, qn, 0.0).astype(_BF16)
        qs_ref[1, :, sl] = jnp.where(lo, 0.0, qn).astype(_BF16)
    zk = proj(1)
    for h in range(N_HEADS):
        sl = slice(h * LANES, (h + 1) * LANES)
        kn = head_norm(zk[:, sl], gk_ref[...])
        kf_ref[:, sl] = kn
        kb_ref[:, sl] = kn.astype(_BF16)
    zv = proj(2)
    vf_ref[...] = zv
    vb_ref[...] = zv.astype(_BF16)
    u_ref[...] = proj(3) * jax.nn.sigmoid(proj(4))
    gt_ref[:, :D_MODEL] = jax.nn.sigmoid(proj(5) + bg_ref[:, :D_MODEL])
    gt_ref[:, D_MODEL:] = jax.nn.sigmoid(proj(6) + bg_ref[:, D_MODEL:])


def _inproj(x, g_mix, w_in_bf, b_gate, gq, gk, tm):
    n = x.shape[0]
    row = lambda w: pl.BlockSpec((tm, w), lambda i: (i, 0))
    return pl.pallas_call(
        _inproj_kernel,
        grid=(n // tm,),
        in_specs=[row(D_MODEL), _resident((1, D_MODEL)), _resident(w_in_bf.shape),
                  _resident((1, 2 * D_MODEL)), _resident((1, LANES)), _resident((1, LANES))],
        out_specs=[pl.BlockSpec((2, tm, D_MODEL), lambda i: (0, i, 0)),
                   row(D_MODEL), row(D_MODEL), row(D_MODEL), row(D_MODEL), row(D_MODEL),
                   row(2 * D_MODEL)],
        out_shape=[jax.ShapeDtypeStruct((2, n, D_MODEL), _BF16),
                   jax.ShapeDtypeStruct((n, D_MODEL), _F32),
                   jax.ShapeDtypeStruct((n, D_MODEL), _BF16),
                   jax.ShapeDtypeStruct((n, D_MODEL), _F32),
                   jax.ShapeDtypeStruct((n, D_MODEL), _BF16),
                   jax.ShapeDtypeStruct((n, D_MODEL), _F32),
                   jax.ShapeDtypeStruct((n, 2 * D_MODEL), _F32)],
        compiler_params=_params("parallel"),
        name="inproj",
    )(x, g_mix, w_in_bf, b_gate, gq, gk)


def _lambda(lam_ref, lambda_init):
    lam = lam_ref[...]
    a = jnp.sum(lam[0:1] * lam[1:2], axis=-1, keepdims=True)
    b = jnp.sum(lam[2:3] * lam[3:4], axis=-1, keepdims=True)
    return jnp.exp(a) - jnp.exp(b) + lambda_init


def _attn_finish(acc, l, tq, lam, gsub_ref, lambda_init):
    o = acc / l
    o = o[:tq] - lam * o[tq:]
    o = o * lax.rsqrt(jnp.mean(o * o, axis=-1, keepdims=True) + EPS) * gsub_ref[...]
    return (o * (1.0 - lambda_init)).astype(_BF16)


def _qk(q, k):
    return lax.dot_general(q, k, (((1,), (1,)), ((), ())), preferred_element_type=_F32)


def _prompt_attn_kernel(q_ref, k_ref, v_ref, lam_ref, gsub_ref, o_ref, *, tq, lambda_init):
    qi = pl.program_id(2)
    q = jnp.concatenate([q_ref[0], q_ref[1]], axis=0)

    def step(j, carry, masked):
        m, l, acc = carry
        off = pl.multiple_of(j * tq, tq)
        s = _qk(q, k_ref[pl.ds(off, tq), :])
        if masked:
            qc = (lax.broadcasted_iota(jnp.int32, s.shape, 0) % tq) // CHUNK
            kc = lax.broadcasted_iota(jnp.int32, s.shape, 1) // CHUNK
            s = jnp.where(kc <= qc, s, NEG_INF)
        m_new = jnp.maximum(m, jnp.max(s, axis=-1, keepdims=True))
        alpha = jnp.exp(m - m_new)
        p = jnp.exp(s - m_new)
        l = alpha * l + jnp.sum(p, axis=-1, keepdims=True)
        acc = alpha * acc + jnp.dot(p.astype(_BF16), v_ref[pl.ds(off, tq), :],
                                    preferred_element_type=_F32)
        return m_new, l, acc

    init = (jnp.full((2 * tq, 1), NEG_INF, _F32), jnp.zeros((2 * tq, 1), _F32),
            jnp.zeros((2 * tq, V_DIM), _F32))
    carry = lax.fori_loop(0, qi, lambda j, c: step(j, c, False), init)
    _, l, acc = step(qi, carry, True)
    o_ref[...] = _attn_finish(acc, l, tq, _lambda(lam_ref, lambda_init), gsub_ref, lambda_init)


def _prompt_attention(qs, kb, vb, lam4, gsub, batch, seq, tq, lambda_init):
    n = batch * seq
    nq = seq // tq
    return pl.pallas_call(
        functools.partial(_prompt_attn_kernel, tq=tq, lambda_init=lambda_init),
        grid=(batch, N_HEADS, nq),
        in_specs=[pl.BlockSpec((2, tq, LANES), lambda b, h, i: (0, b * nq + i, h)),
                  pl.BlockSpec((seq, LANES), lambda b, h, i: (b, h)),
                  pl.BlockSpec((seq, LANES), lambda b, h, i: (b, h)),
                  _resident((4, HEAD_DIM)), _resident((1, V_DIM))],
        out_specs=pl.BlockSpec((tq, LANES), lambda b, h, i: (b * nq + i, h)),
        out_shape=jax.ShapeDtypeStruct((n, D_MODEL), _BF16),
        compiler_params=_params("parallel", "parallel", "arbitrary"),
        name="prompt_attn",
    )(qs, kb, vb, lam4, gsub)


def _decode_attn_kernel(q_ref, kn_ref, vn_ref, kc_ref, vc_ref, lam_ref, gsub_ref, o_ref, *, tq, lambda_init):
    q = jnp.concatenate([q_ref[0], q_ref[1]], axis=0)
    s_c = _qk(q, kc_ref[0].astype(_BF16))
    s_n = _qk(q, kn_ref[...])
    m = jnp.maximum(jnp.max(s_c, axis=-1, keepdims=True), jnp.max(s_n, axis=-1, keepdims=True))
    p_c = jnp.exp(s_c - m)
    p_n = jnp.exp(s_n - m)
    l = jnp.sum(p_c, axis=-1, keepdims=True) + jnp.sum(p_n, axis=-1, keepdims=True)
    acc = (jnp.dot(p_c.astype(_BF16), vc_ref[0].astype(_BF16), preferred_element_type=_F32)
           + jnp.dot(p_n.astype(_BF16), vn_ref[...], preferred_element_type=_F32))
    o_ref[...] = _attn_finish(acc, l, tq, _lambda(lam_ref, lambda_init), gsub_ref, lambda_init)


def _decode_attention(qs, kb, vb, cache_k, cache_v, lam4, gsub, batch, seq, lambda_init):
    past = cache_k.shape[1]
    return pl.pallas_call(
        functools.partial(_decode_attn_kernel, tq=seq, lambda_init=lambda_init),
        grid=(batch, N_HEADS),
        in_specs=[pl.BlockSpec((2, seq, LANES), lambda b, h: (0, b, h)),
                  pl.BlockSpec((seq, LANES), lambda b, h: (b, h)),
                  pl.BlockSpec((seq, LANES), lambda b, h: (b, h)),
                  pl.BlockSpec((1, past, LANES), lambda b, h: (b, 0, h)),
                  pl.BlockSpec((1, past, LANES), lambda b, h: (b, 0, h)),
                  _resident((4, HEAD_DIM)), _resident((1, V_DIM))],
        out_specs=pl.BlockSpec((seq, LANES), lambda b, h: (b, h)),
        out_shape=jax.ShapeDtypeStruct((batch * seq, D_MODEL), _BF16),
        compiler_params=_params("parallel", "parallel"),
        name="decode_attn",
    )(qs, kb, vb, cache_k, cache_v, lam4, gsub)


CONV_ROWS = 32


def _conv_kernel(u_ref, hist_ref, w_ref, b_ref, g_ref, beta_ref, y_ref, win_ref, *, tt):
    @pl.when(pl.program_id(1) == 0)
    def _():
        win_ref[0:HIST_ROWS, :] = hist_ref[0]

    win_ref[HIST_ROWS:HIST_ROWS + tt, :] = u_ref[0]
    first = HIST_ROWS - (CONV_K - 1)
    rows = min(CONV_ROWS, tt)
    for r0 in range(0, tt, rows):
        acc = jnp.broadcast_to(b_ref[...], (rows, D_MODEL))
        for j in range(CONV_K):
            acc = acc + w_ref[j:j + 1, :] * win_ref[first + r0 + j:first + r0 + j + rows, :]
        mu = jnp.mean(acc, axis=-1, keepdims=True)
        xc = acc - mu
        y = xc * lax.rsqrt(jnp.mean(xc * xc, axis=-1, keepdims=True) + EPS)
        y = y * g_ref[...] + beta_ref[...]
        y_ref[0, r0:r0 + rows, :] = (y * jax.nn.sigmoid(y)).astype(_BF16)
    win_ref[0:HIST_ROWS, :] = win_ref[tt:tt + HIST_ROWS, :]


def _conv_branch(u, hist, w_dw, b_dw, g_cln, b_cln, tt):
    batch, seq, _ = u.shape
    return pl.pallas_call(
        functools.partial(_conv_kernel, tt=tt),
        grid=(batch, seq // tt),
        in_specs=[pl.BlockSpec((1, tt, D_MODEL), lambda b, i: (b, i, 0)),
                  pl.BlockSpec((1, HIST_ROWS, D_MODEL), lambda b, i: (b, 0, 0)),
                  _resident((HIST_ROWS, D_MODEL)), _resident((1, D_MODEL)),
                  _resident((1, D_MODEL)), _resident((1, D_MODEL))],
        out_specs=pl.BlockSpec((1, tt, D_MODEL), lambda b, i: (b, i, 0)),
        out_shape=jax.ShapeDtypeStruct((batch, seq, D_MODEL), _BF16),
        scratch_shapes=[pltpu.VMEM((HIST_ROWS + max(tt, HIST_ROWS), D_MODEL), _F32)],
        compiler_params=_params("parallel", "arbitrary"),
        name="conv_branch",
    )(u, hist, w_dw, b_dw, g_cln, b_cln)


def _merge_kernel(o_ref, c_ref, gt_ref, x_ref, wa_ref, wc_ref, wo_ref, gffn_ref, h_ref, hn_ref):
    attn = jnp.dot(o_ref[...], wa_ref[...], preferred_element_type=_F32)
    conv = jnp.dot(c_ref[...], wc_ref[...], preferred_element_type=_F32)
    mix = gt_ref[:, :D_MODEL] * attn + gt_ref[:, D_MODEL:] * conv
    h = x_ref[...] + jnp.dot(mix.astype(_BF16), wo_ref[...], preferred_element_type=_F32)
    h_ref[...] = h
    hn = h * lax.rsqrt(jnp.mean(h * h, axis=-1, keepdims=True) + EPS) * gffn_ref[...]
    hn_ref[...] = hn.astype(_BF16).astype(_F32)


def _merge(o, c, gt, x, wa, wc, wo, g_ffn, tm):
    n = x.shape[0]
    row = lambda w: pl.BlockSpec((tm, w), lambda i: (i, 0))
    sq = (D_MODEL, D_MODEL)
    return pl.pallas_call(
        _merge_kernel,
        grid=(n // tm,),
        in_specs=[row(D_MODEL), row(D_MODEL), row(2 * D_MODEL), row(D_MODEL),
                  _resident(sq), _resident(sq), _resident(sq), _resident((1, D_MODEL))],
        out_specs=[row(D_MODEL), row(D_MODEL)],
        out_shape=[jax.ShapeDtypeStruct((n, D_MODEL), _F32), jax.ShapeDtypeStruct((n, D_MODEL), _F32)],
        compiler_params=_params("parallel"),
        name="merge",
    )(o, c, gt, x, wa, wc, wo, g_ffn)


_KEY_SENTINEL = N_KEYS * N_KEYS


def _topk_rows(s, keys, k):
    vals, sels = [], []
    for r in range(k):
        m = jnp.max(s, axis=0, keepdims=True)
        sel = jnp.min(jnp.where(s == m, keys, _KEY_SENTINEL), axis=0, keepdims=True)
        vals.append(m)
        sels.append(sel)
        if r + 1 < k:
            s = jnp.where(keys == sel, -jnp.inf, s)
    return jnp.concatenate(vals, axis=0), jnp.concatenate(sels, axis=0)


def _route_kernel(hn_ref, wpq_ref, sk_ref, idx_ref, g_ref, *, tm):
    q = jnp.dot(hn_ref[...].astype(_BF16), wpq_ref[...], preferred_element_type=_F32).astype(_BF16)
    key_iota = lax.broadcasted_iota(jnp.int32, (N_KEYS, LANES), 0)
    for lt in range(tm // LANES):
        rows = slice(lt * LANES, (lt + 1) * LANES)
        idx_parts, g_parts = [], []
        for h in range(PEER_HEADS):
            top = []
            for p in range(2):
                hp = 2 * h + p
                st = _qk(sk_ref[hp], q[rows, hp * LANES:(hp + 1) * LANES])
                top.append(_topk_rows(st, key_iota, PEER_TOPK))
            (ts0, ti0), (ts1, ti1) = top
            cand_s = jnp.concatenate([ts0[i:i + 1] + ts1 for i in range(PEER_TOPK)], axis=0)
            cand_i = jnp.concatenate([ti0[i:i + 1] * N_KEYS + ti1 for i in range(PEER_TOPK)], axis=0)
            best_s, best_i = _topk_rows(cand_s, cand_i, PEER_TOPK)
            e = jnp.exp(best_s - best_s[0:1])
            g_parts.append(e / jnp.sum(e, axis=0, keepdims=True))
            idx_parts.append(best_i)
        idx_ref[rows, :] = jnp.concatenate(idx_parts, axis=0).T
        g_ref[rows, :] = jnp.concatenate(g_parts, axis=0).T


def _route(hn, w_pq_bf, sk_bf, tm):
    n = hn.shape[0]
    width = PEER_HEADS * PEER_TOPK
    return pl.pallas_call(
        functools.partial(_route_kernel, tm=tm),
        grid=(n // tm,),
        in_specs=[pl.BlockSpec((tm, D_MODEL), lambda i: (i, 0)),
                  _resident(w_pq_bf.shape), _resident(sk_bf.shape)],
        out_specs=[pl.BlockSpec((tm, width), lambda i: (i, 0)), pl.BlockSpec((tm, width), lambda i: (i, 0))],
        out_shape=[jax.ShapeDtypeStruct((n, width), jnp.int32), jax.ShapeDtypeStruct((n, width), _F32)],
        compiler_params=_params("parallel"),
        name="peer_route",
    )(hn, w_pq_bf, sk_bf)


HALF = D_MODEL // 2
SC_CORES = 2
SC_SUBCORES = 16
SC_LANES = 16
SC_TOKENS = 8
GELU_C = math.sqrt(2.0 / math.pi)


def _pack_experts(expert_u, expert_v):
    def words(t):
        b = lax.bitcast_convert_type(t.astype(_BF16), jnp.uint16).astype(jnp.uint32)
        return b[:, :HALF] | (b[:, HALF:] << 16)
    return jnp.concatenate([words(expert_u), words(expert_v)], axis=1)


def _retrieve_body(idx_hbm, g_hbm, x_hbm, h_hbm, tab_hbm, y_hbm, idx_v, g_v, x_v, o_v, rows_v, tr_v, sems, *,
                   blocks_per_worker):
    tb, lanes = SC_TOKENS, SC_LANES
    wid = lax.axis_index("s") * SC_CORES + lax.axis_index("c")
    lane = lax.iota(jnp.int32, lanes)
    chunks = tb * PEER_HEADS

    def gather(q, slot):
        return pltpu.make_async_copy(tab_hbm.at[idx_v.at[q]], rows_v.at[slot], sems.at[slot])

    def unpack(words):
        return (plsc.bitcast(words << 16, _F32), plsc.bitcast(words & jnp.uint32(0xFFFF0000), _F32))

    def head_chunk(q, slot):
        tok = q // PEER_HEADS

        def dot_step(kc, accs):
            off = pl.multiple_of(kc * lanes, lanes)
            x_lo = x_v[tok, pl.ds(off, lanes)]
            x_hi = x_v[tok, pl.ds(off + HALF, lanes)]
            out = []
            for r in range(PEER_TOPK):
                lo, hi = unpack(rows_v[slot, r, pl.ds(off, lanes)])
                out.append(accs[r] + lo * x_lo + hi * x_hi)
            return tuple(out)

        zeros = tuple(jnp.zeros((lanes,), _F32) for _ in range(PEER_TOPK))
        accs = plsc.parallel_loop(0, HALF // lanes, carry=zeros)(dot_step)
        for r in range(PEER_TOPK):
            tr_v[r, :] = accs[r]
        d = jnp.zeros((lanes,), _F32)
        for c in range(lanes):
            d = d + plsc.load_gather(tr_v, [lane, jnp.full((lanes,), c, jnp.int32)])
        z = GELU_C * (d + 0.044715 * d * d * d)
        act = 0.5 * d * (2.0 - 2.0 / (jnp.exp(2.0 * z) + 1.0))
        wgt = g_v[q, :] * act
        splat = [wgt.at[jnp.full((lanes,), r, jnp.int32)].get(mode="promise_in_bounds")
                 for r in range(PEER_TOPK)]

        def acc_step(kc):
            off = pl.multiple_of(kc * lanes, lanes)
            a_lo = jnp.zeros((lanes,), _F32)
            a_hi = jnp.zeros((lanes,), _F32)
            for r in range(PEER_TOPK):
                lo, hi = unpack(rows_v[slot, r, pl.ds(off + HALF, lanes)])
                a_lo = a_lo + splat[r] * lo
                a_hi = a_hi + splat[r] * hi
            plsc.addupdate(o_v.at[tok, pl.ds(off, lanes)], a_lo)
            plsc.addupdate(o_v.at[tok, pl.ds(off + HALF, lanes)], a_hi)

        plsc.parallel_loop(0, HALF // lanes)(acc_step)

    def block(b, _):
        t0 = (wid * blocks_per_worker + b) * tb
        pltpu.sync_copy(idx_hbm.at[pl.ds(t0 * PEER_HEADS, chunks)], idx_v)
        pltpu.sync_copy(g_hbm.at[pl.ds(t0 * PEER_HEADS, chunks)], g_v)
        pltpu.sync_copy(x_hbm.at[pl.ds(t0, tb)], x_v)
        pltpu.sync_copy(h_hbm.at[pl.ds(t0, tb)], o_v)
        gather(0, 0).start()

        def pair(p, _):
            for s in range(2):
                q = p * 2 + s

                @pl.when(q + 1 < chunks)
                def _():
                    gather(q + 1, 1 - s).start()

                gather(q, s).wait()
                head_chunk(q, s)
            return 0

        lax.fori_loop(0, chunks // 2, pair, 0)
        pltpu.sync_copy(o_v, y_hbm.at[pl.ds(t0, tb)])
        return 0

    lax.fori_loop(0, blocks_per_worker, block, 0)


def _retrieve(idx, table, g, hn, h):
    n = h.shape[0]
    workers = SC_CORES * SC_SUBCORES
    assert n % (workers * SC_TOKENS) == 0, n
    mesh = plsc.VectorSubcoreMesh(core_axis_name="c", subcore_axis_name="s")
    chunks = SC_TOKENS * PEER_HEADS
    call = pl.kernel(
        functools.partial(_retrieve_body, blocks_per_worker=n // (workers * SC_TOKENS)),
        out_type=jax.ShapeDtypeStruct((n, D_MODEL), _F32),
        mesh=mesh,
        scratch_types=[pltpu.VMEM((chunks, PEER_TOPK), jnp.int32),
                       pltpu.VMEM((chunks, PEER_TOPK), _F32),
                       pltpu.VMEM((SC_TOKENS, D_MODEL), _F32),
                       pltpu.VMEM((SC_TOKENS, D_MODEL), _F32),
                       pltpu.VMEM((2, PEER_TOPK, D_MODEL), jnp.uint32),
                       pltpu.VMEM((PEER_TOPK, SC_LANES), _F32),
                       pltpu.SemaphoreType.DMA((2,))],
        compiler_params=pltpu.CompilerParams(needs_layout_passes=False),
        name="peer_retrieve",
    )
    return call(idx.reshape(n * PEER_HEADS, PEER_TOPK), g.reshape(n * PEER_HEADS, PEER_TOPK), hn, h, table)


def _row_tile(n):
    return min(256, n)


PROMPT_GROUPS = 4


def _layer(x, cache_k, cache_v, hist, lambda_init, wts, groups):
    per = x.shape[0] // groups
    outs = []
    for gi in range(groups):
        sl = slice(gi * per, (gi + 1) * per)
        ck = None if cache_k is None else cache_k[sl]
        cv = None if cache_v is None else cache_v[sl]
        outs.append(_layer_group(x[sl], ck, cv, hist[sl], lambda_init, wts))
    if groups == 1:
        return outs[0]
    return tuple(jnp.concatenate(parts, axis=0) for parts in zip(*outs))


def _layer_group(x, cache_k, cache_v, hist, lambda_init, wts):
    batch, seq, _ = x.shape
    n = batch * seq
    xf = x.reshape(n, D_MODEL)
    tm = _row_tile(n)
    qs, kf, kb, vf, vb, u, gt = _inproj(xf, wts["g_mix"], wts["w_in"], wts["b_gate"], wts["gq"], wts["gk"], tm)
    if cache_k is None:
        o = _prompt_attention(qs, kb, vb, wts["lam4"], wts["g_sub"], batch, seq, min(256, seq), lambda_init)
    else:
        o = _decode_attention(qs, kb, vb, cache_k, cache_v, wts["lam4"], wts["g_sub"], batch, seq, lambda_init)
    u3 = u.reshape(batch, seq, D_MODEL)
    c = _conv_branch(u3, hist, wts["w_dw"], wts["b_dw"], wts["g_cln"], wts["b_cln"], min(256, seq))
    h, hn = _merge(o, c.reshape(n, D_MODEL), gt, xf, wts["w_attn_out"], wts["w_conv_out"], wts["w_out"],
                   wts["g_ffn"], tm)
    idx, g = _route(hn, wts["w_pq"], wts["sub_keys"], tm)
    y = _retrieve(idx, wts["table"], g, hn, h)
    keep = CONV_K - 1
    if seq >= keep:
        conv_state = u3[:, seq - keep:]
    else:
        conv_state = jnp.concatenate([hist[:, HIST_ROWS - keep + seq:], u3], axis=1)
    return (y.reshape(batch, seq, D_MODEL), kf.reshape(batch, seq, N_HEADS, 2, HEAD_DIM),
            vf.reshape(batch, seq, N_HEADS, V_DIM), conv_state)


def _layer_weights(l, g_mix, w_in, b_gate, g_qn, g_kn, lam_q1, lam_k1, lam_q2, lam_k2, g_sub,
                   w_attn_out, w_dw, b_dw, g_cln, b_cln, w_conv_out, w_out, g_ffn, w_pq, sub_keys,
                   expert_u, expert_v):
    row = lambda a: a[l].reshape(1, -1)
    return dict(
        g_mix=row(g_mix), w_in=w_in[l].astype(_BF16), b_gate=row(b_gate),
        gq=jnp.tile(g_qn[l], 2).reshape(1, LANES), gk=jnp.tile(g_kn[l], 2).reshape(1, LANES),
        lam4=jnp.stack([lam_q1[l], lam_k1[l], lam_q2[l], lam_k2[l]]), g_sub=row(g_sub),
        w_attn_out=w_attn_out[l].astype(_BF16),
        w_dw=jnp.pad(w_dw[l].reshape(CONV_K, D_MODEL), ((0, HIST_ROWS - CONV_K), (0, 0))),
        b_dw=row(b_dw), g_cln=row(g_cln), b_cln=row(b_cln),
        w_conv_out=w_conv_out[l].astype(_BF16), w_out=w_out[l].astype(_BF16), g_ffn=row(g_ffn),
        w_pq=w_pq[l].astype(_BF16),
        sub_keys=sub_keys[l].reshape(2 * PEER_HEADS, N_KEYS, -1).astype(_BF16),
        table=_pack_experts(expert_u[l], expert_v[l]),
    )


def kernel(x_prompt, x_sample, cache_k, cache_v, cache_conv, g_mix, w_in, b_gate, g_qn, g_kn, lam_q1, lam_k1, lam_q2, lam_k2, g_sub, w_attn_out, w_dw, b_dw, g_cln, b_cln, w_conv_out, w_out, g_ffn, w_pq, sub_keys, expert_u, expert_v):
    depth = w_in.shape[0]
    yp, ys = x_prompt, x_sample
    pad_hist = lambda hst: jnp.pad(hst, ((0, 0), (HIST_ROWS - (CONV_K - 1), 0), (0, 0)))
    zero_hist = jnp.zeros((x_prompt.shape[0], HIST_ROWS, D_MODEL), x_prompt.dtype)
    outs = [[] for _ in range(6)]
    for l in range(depth):
        lambda_init = 0.8 - 0.6 * math.exp(-0.3 * l)
        wts = _layer_weights(l, g_mix, w_in, b_gate, g_qn, g_kn, lam_q1, lam_k1, lam_q2, lam_k2, g_sub,
                             w_attn_out, w_dw, b_dw, g_cln, b_cln, w_conv_out, w_out, g_ffn, w_pq,
                             sub_keys, expert_u, expert_v)
        yp, kp, vp, cp = _layer(yp, None, None, zero_hist, lambda_init, wts,
                                groups=math.gcd(PROMPT_GROUPS, yp.shape[0]))
        db, past = cache_k.shape[1], cache_k.shape[2]
        ys, kn, vn, cn = _layer(ys, cache_k[l].reshape(db, past, D_MODEL), cache_v[l].reshape(db, past, D_MODEL),
                                pad_hist(cache_conv[l]), lambda_init, wts, groups=1)
        for lst, val in zip(outs, (kp, vp, cp, kn, vn, cn)):
            lst.append(val)
    return (yp, ys) + tuple(jnp.stack(lst) for lst in outs)
```

```python
import functools
import math

import jax
import jax.numpy as jnp
from jax import lax
from jax.experimental import pallas as pl
from jax.experimental.pallas import tpu as pltpu
from jax.experimental.pallas import tpu_sc as plsc

D_MODEL = 1024
CHUNK = 64
N_HEADS = 8
HEAD_DIM = 64
V_DIM = 2 * HEAD_DIM
ATTN_SCALE = HEAD_DIM ** -0.5
CONV_K = 31
PEER_HEADS = 8
N_KEYS = 128
PEER_TOPK = 16
EPS = 1e-6
NEG_INF = -1e30
HALF = D_MODEL // 2

LANES = 128
HIST_ROWS = 32
VMEM_LIMIT = 48 * 1024 * 1024

_BF16 = jnp.bfloat16
_F32 = jnp.float32


def _resident(shape):
    nd = len(shape)
    return pl.BlockSpec(shape, lambda *_: (0,) * nd, pipeline_mode=pl.Buffered(1))


def _params(*sem):
    return pltpu.CompilerParams(dimension_semantics=sem, vmem_limit_bytes=VMEM_LIMIT)


def _inproj_kernel(x_ref, gmix_ref, w_ref, bg_ref, gq_ref, gk_ref,
                   qs_ref, kf_ref, kb_ref, vf_ref, vb_ref, u_ref, gt_ref, *, v_transposed):
    x = x_ref[...]
    ms = jnp.mean(x * x, axis=-1, keepdims=True)
    xn = (x * lax.rsqrt(ms + EPS) * gmix_ref[...]).astype(_BF16)

    def proj(c):
        return jnp.dot(xn, w_ref[:, c * D_MODEL:(c + 1) * D_MODEL], preferred_element_type=_F32)

    lo = lax.broadcasted_iota(jnp.int32, (1, LANES), 1) < HEAD_DIM

    def head_norm(zh, g):
        sq = zh * zh
        ss_lo = jnp.sum(jnp.where(lo, sq, 0.0), axis=-1, keepdims=True)
        ss_hi = jnp.sum(jnp.where(lo, 0.0, sq), axis=-1, keepdims=True)
        r = jnp.where(lo, lax.rsqrt(ss_lo / HEAD_DIM + EPS), lax.rsqrt(ss_hi / HEAD_DIM + EPS))
        return zh * r * g

    zq = proj(0)
    for h in range(N_HEADS):
        sl = slice(h * LANES, (h + 1) * LANES)
        qn = head_norm(zq[:, sl], gq_ref[...]) * ATTN_SCALE
        qs_ref[0, :, sl] = jnp.where(lo, qn, 0.0).astype(_BF16)
        qs_ref[1, :, sl] = jnp.where(lo, 0.0, qn).astype(_BF16)
    zk = proj(1)
    for h in range(N_HEADS):
        sl = slice(h * LANES, (h + 1) * LANES)
        kn = head_norm(zk[:, sl], gk_ref[...])
        kf_ref[:, sl] = kn
        kb_ref[:, sl] = kn.astype(_BF16)
    zv = proj(2)
    vf_ref[...] = zv
    if v_transposed:
        vb_ref[0] = zv.T.astype(_BF16)
    else:
        vb_ref[...] = zv.astype(_BF16)
    u_ref[...] = proj(3) * jax.nn.sigmoid(proj(4))
    gt_ref[:, :D_MODEL] = jax.nn.sigmoid(proj(5) + bg_ref[:, :D_MODEL])
    gt_ref[:, D_MODEL:] = jax.nn.sigmoid(proj(6) + bg_ref[:, D_MODEL:])


def _inproj(x, g_mix, w_in_bf, b_gate, gq, gk, tm, v_transposed):
    n = x.shape[0]
    row = lambda w: pl.BlockSpec((tm, w), lambda i: (i, 0))
    if v_transposed:
        vb_spec = pl.BlockSpec((1, D_MODEL, tm), lambda i: (i, 0, 0))
        vb_shape = jax.ShapeDtypeStruct((n // tm, D_MODEL, tm), _BF16)
    else:
        vb_spec, vb_shape = row(D_MODEL), jax.ShapeDtypeStruct((n, D_MODEL), _BF16)
    return pl.pallas_call(
        functools.partial(_inproj_kernel, v_transposed=v_transposed),
        grid=(n // tm,),
        in_specs=[row(D_MODEL), _resident((1, D_MODEL)), _resident(w_in_bf.shape),
                  _resident((1, 2 * D_MODEL)), _resident((1, LANES)), _resident((1, LANES))],
        out_specs=[pl.BlockSpec((2, tm, D_MODEL), lambda i: (0, i, 0)),
                   row(D_MODEL), row(D_MODEL), row(D_MODEL), vb_spec, row(D_MODEL),
                   row(2 * D_MODEL)],
        out_shape=[jax.ShapeDtypeStruct((2, n, D_MODEL), _BF16),
                   jax.ShapeDtypeStruct((n, D_MODEL), _F32),
                   jax.ShapeDtypeStruct((n, D_MODEL), _BF16),
                   jax.ShapeDtypeStruct((n, D_MODEL), _F32),
                   vb_shape,
                   jax.ShapeDtypeStruct((n, D_MODEL), _F32),
                   jax.ShapeDtypeStruct((n, 2 * D_MODEL), _F32)],
        compiler_params=_params("parallel"),
        name="inproj",
    )(x, g_mix, w_in_bf, b_gate, gq, gk)


def _lambda(lam_ref, lambda_init):
    lam = lam_ref[...]
    a = jnp.sum(lam[0:1] * lam[1:2], axis=-1, keepdims=True)
    b = jnp.sum(lam[2:3] * lam[3:4], axis=-1, keepdims=True)
    return jnp.exp(a) - jnp.exp(b) + lambda_init


def _attn_finish(acc, l, tq, lam, gsub_ref, lambda_init):
    o = acc / l
    o = o[:tq] - lam * o[tq:]
    o = o * lax.rsqrt(jnp.mean(o * o, axis=-1, keepdims=True) + EPS) * gsub_ref[...]
    return (o * (1.0 - lambda_init)).astype(_BF16)


def _qk(q, k):
    return lax.dot_general(q, k, (((1,), (1,)), ((), ())), preferred_element_type=_F32)


ATTN_HEADS_PER_STEP = 4


def _prompt_attn_kernel(q_ref, k_ref, vt_ref, lam_ref, gsub_ref, o_ref, *, tq, lambda_init):
    qi = pl.program_id(2)
    heads = range(ATTN_HEADS_PER_STEP)
    hs = lambda h: slice(h * LANES, (h + 1) * LANES)
    q = [jnp.concatenate([q_ref[0, :, hs(h)], q_ref[1, :, hs(h)]], axis=0) for h in heads]

    def step(j, carry, masked):
        off = pl.multiple_of(j * tq, tq)
        out = []
        for h in heads:
            m, l, acc = carry[h]
            s = _qk(k_ref[pl.ds(off, tq), hs(h)], q[h])
            if masked:
                kc = lax.broadcasted_iota(jnp.int32, s.shape, 0) // CHUNK
                qc = (lax.broadcasted_iota(jnp.int32, s.shape, 1) % tq) // CHUNK
                s = jnp.where(kc <= qc, s, NEG_INF)
            m_new = jnp.maximum(m, jnp.max(s, axis=0, keepdims=True))
            alpha = jnp.exp(m - m_new)
            p = jnp.exp(s - m_new)
            l = alpha * l + jnp.sum(p, axis=0, keepdims=True)
            acc = alpha * acc + jnp.dot(vt_ref[j, hs(h), :], p.astype(_BF16), preferred_element_type=_F32)
            out.append((m_new, l, acc))
        return tuple(out)

    init = tuple((jnp.full((1, 2 * tq), NEG_INF, _F32), jnp.zeros((1, 2 * tq), _F32),
                  jnp.zeros((V_DIM, 2 * tq), _F32)) for _ in heads)
    carry = lax.fori_loop(0, qi, lambda j, c: step(j, c, False), init)
    carry = step(qi, carry, True)
    lam = _lambda(lam_ref, lambda_init)
    for h in heads:
        _, l, acc = carry[h]
        o = acc / l
        o = o[:, :tq] - lam * o[:, tq:]
        o = o * lax.rsqrt(jnp.mean(o * o, axis=0, keepdims=True) + EPS) * gsub_ref[...]
        o_ref[:, hs(h)] = (o * (1.0 - lambda_init)).T.astype(_BF16)


def _prompt_attention(qs, kb, vt, lam4, gsub_col, batch, seq, tq, lambda_init):
    n = batch * seq
    nq = seq // tq
    hw = ATTN_HEADS_PER_STEP * LANES
    return pl.pallas_call(
        functools.partial(_prompt_attn_kernel, tq=tq, lambda_init=lambda_init),
        grid=(batch, N_HEADS // ATTN_HEADS_PER_STEP, nq),
        in_specs=[pl.BlockSpec((2, tq, hw), lambda b, h, i: (0, b * nq + i, h)),
                  pl.BlockSpec((seq, hw), lambda b, h, i: (b, h)),
                  pl.BlockSpec((nq, hw, tq), lambda b, h, i: (b, h, 0)),
                  _resident((4, HEAD_DIM)), _resident((V_DIM, 1))],
        out_specs=pl.BlockSpec((tq, hw), lambda b, h, i: (b * nq + i, h)),
        out_shape=jax.ShapeDtypeStruct((n, D_MODEL), _BF16),
        compiler_params=_params("parallel", "parallel", "arbitrary"),
        name="prompt_attn",
    )(qs, kb, vt, lam4, gsub_col)


def _decode_attn_kernel(q_ref, kn_ref, vn_ref, kc_ref, vc_ref, lam_ref, gsub_ref, o_ref, *, tq, lambda_init):
    q = jnp.concatenate([q_ref[0], q_ref[1]], axis=0)
    s_c = _qk(q, kc_ref[0].astype(_BF16))
    s_n = _qk(q, kn_ref[...])
    m = jnp.maximum(jnp.max(s_c, axis=-1, keepdims=True), jnp.max(s_n, axis=-1, keepdims=True))
    p_c = jnp.exp(s_c - m)
    p_n = jnp.exp(s_n - m)
    l = jnp.sum(p_c, axis=-1, keepdims=True) + jnp.sum(p_n, axis=-1, keepdims=True)
    acc = (jnp.dot(p_c.astype(_BF16), vc_ref[0].astype(_BF16), preferred_element_type=_F32)
           + jnp.dot(p_n.astype(_BF16), vn_ref[...], preferred_element_type=_F32))
    o_ref[...] = _attn_finish(acc, l, tq, _lambda(lam_ref, lambda_init), gsub_ref, lambda_init)


def _decode_attention(qs, kb, vb, cache_k, cache_v, lam4, gsub, batch, seq, lambda_init):
    past = cache_k.shape[1]
    return pl.pallas_call(
        functools.partial(_decode_attn_kernel, tq=seq, lambda_init=lambda_init),
        grid=(batch, N_HEADS),
        in_specs=[pl.BlockSpec((2, seq, LANES), lambda b, h: (0, b, h)),
                  pl.BlockSpec((seq, LANES), lambda b, h: (b, h)),
                  pl.BlockSpec((seq, LANES), lambda b, h: (b, h)),
                  pl.BlockSpec((1, past, LANES), lambda b, h: (b, 0, h)),
                  pl.BlockSpec((1, past, LANES), lambda b, h: (b, 0, h)),
                  _resident((4, HEAD_DIM)), _resident((1, V_DIM))],
        out_specs=pl.BlockSpec((seq, LANES), lambda b, h: (b, h)),
        out_shape=jax.ShapeDtypeStruct((batch * seq, D_MODEL), _BF16),
        compiler_params=_params("parallel", "parallel"),
        name="decode_attn",
    )(qs, kb, vb, cache_k, cache_v, lam4, gsub)


CONV_ROWS = 32


def _conv_kernel(u_ref, hist_ref, w_ref, b_ref, g_ref, beta_ref, y_ref, win_ref, *, tt):
    @pl.when(pl.program_id(1) == 0)
    def _():
        win_ref[0:HIST_ROWS, :] = hist_ref[0]

    win_ref[HIST_ROWS:HIST_ROWS + tt, :] = u_ref[0]
    first = HIST_ROWS - (CONV_K - 1)
    rows = min(CONV_ROWS, tt)
    for r0 in range(0, tt, rows):
        acc = jnp.broadcast_to(b_ref[...], (rows, D_MODEL))
        for j in range(CONV_K):
            acc = acc + w_ref[j:j + 1, :] * win_ref[first + r0 + j:first + r0 + j + rows, :]
        mu = jnp.mean(acc, axis=-1, keepdims=True)
        xc = acc - mu
        y = xc * lax.rsqrt(jnp.mean(xc * xc, axis=-1, keepdims=True) + EPS)
        y = y * g_ref[...] + beta_ref[...]
        y_ref[0, r0:r0 + rows, :] = (y * jax.nn.sigmoid(y)).astype(_BF16)
    win_ref[0:HIST_ROWS, :] = win_ref[tt:tt + HIST_ROWS, :]


def _conv_branch(u, hist, w_dw, b_dw, g_cln, b_cln, tt):
    batch, seq, _ = u.shape
    return pl.pallas_call(
        functools.partial(_conv_kernel, tt=tt),
        grid=(batch, seq // tt),
        in_specs=[pl.BlockSpec((1, tt, D_MODEL), lambda b, i: (b, i, 0)),
                  pl.BlockSpec((1, HIST_ROWS, D_MODEL), lambda b, i: (b, 0, 0)),
                  _resident((HIST_ROWS, D_MODEL)), _resident((1, D_MODEL)),
                  _resident((1, D_MODEL)), _resident((1, D_MODEL))],
        out_specs=pl.BlockSpec((1, tt, D_MODEL), lambda b, i: (b, i, 0)),
        out_shape=jax.ShapeDtypeStruct((batch, seq, D_MODEL), _BF16),
        scratch_shapes=[pltpu.VMEM((HIST_ROWS + max(tt, HIST_ROWS), D_MODEL), _F32)],
        compiler_params=_params("parallel", "arbitrary"),
        name="conv_branch",
    )(u, hist, w_dw, b_dw, g_cln, b_cln)


def _merge_kernel(o_ref, c_ref, gt_ref, x_ref, wa_ref, wc_ref, wo_ref, gffn_ref, h_ref, hn_ref, xp_ref):
    attn = jnp.dot(o_ref[...], wa_ref[...], preferred_element_type=_F32)
    conv = jnp.dot(c_ref[...], wc_ref[...], preferred_element_type=_F32)
    mix = gt_ref[:, :D_MODEL] * attn + gt_ref[:, D_MODEL:] * conv
    h = x_ref[...] + jnp.dot(mix.astype(_BF16), wo_ref[...], preferred_element_type=_F32)
    h_ref[...] = h
    hn = h * lax.rsqrt(jnp.mean(h * h, axis=-1, keepdims=True) + EPS) * gffn_ref[...]
    hn_bf = hn.astype(_BF16)
    hn_ref[...] = hn_bf
    bits = pltpu.bitcast(hn_bf.astype(_F32), jnp.uint32)
    xp_ref[...] = (bits[:, :HALF] >> 16) | (bits[:, HALF:] & jnp.uint32(0xFFFF0000))


def _merge(o, c, gt, x, wa, wc, wo, g_ffn, tm):
    n = x.shape[0]
    row = lambda w: pl.BlockSpec((tm, w), lambda i: (i, 0))
    sq = (D_MODEL, D_MODEL)
    return pl.pallas_call(
        _merge_kernel,
        grid=(n // tm,),
        in_specs=[row(D_MODEL), row(D_MODEL), row(2 * D_MODEL), row(D_MODEL),
                  _resident(sq), _resident(sq), _resident(sq), _resident((1, D_MODEL))],
        out_specs=[row(D_MODEL), row(D_MODEL), row(HALF)],
        out_shape=[jax.ShapeDtypeStruct((n, D_MODEL), _F32), jax.ShapeDtypeStruct((n, D_MODEL), _BF16),
                   jax.ShapeDtypeStruct((n, HALF), jnp.uint32)],
        compiler_params=_params("parallel"),
        name="merge",
    )(o, c, gt, x, wa, wc, wo, g_ffn)


_KEY_SENTINEL = N_KEYS * N_KEYS


def _topk_rows(s, keys, k):
    vals, sels = [], []
    for r in range(k):
        m = jnp.max(s, axis=0, keepdims=True)
        sel = jnp.min(jnp.where(s == m, keys, _KEY_SENTINEL), axis=0, keepdims=True)
        vals.append(m)
        sels.append(sel)
        if r + 1 < k:
            s = jnp.where(keys == sel, -jnp.inf, s)
    return jnp.concatenate(vals, axis=0), jnp.concatenate(sels, axis=0)


def _route_kernel(hn_ref, wpq_ref, sk_ref, idx_ref, g_ref, *, tm):
    q = jnp.dot(hn_ref[...].astype(_BF16), wpq_ref[...], preferred_element_type=_F32).astype(_BF16)
    key_iota = lax.broadcasted_iota(jnp.int32, (N_KEYS, LANES), 0)
    for lt in range(tm // LANES):
        rows = slice(lt * LANES, (lt + 1) * LANES)
        idx_parts, g_parts = [], []
        for h in range(PEER_HEADS):
            top = []
            for p in range(2):
                hp = 2 * h + p
                st = _qk(sk_ref[hp], q[rows, hp * LANES:(hp + 1) * LANES])
                top.append(_topk_rows(st, key_iota, PEER_TOPK))
            (ts0, ti0), (ts1, ti1) = top
            half = PEER_TOPK // 2
            cand_s = jnp.concatenate([ts0[0:1] + ts1]
                                     + [ts0[i:i + 1] + ts1[:half] for i in range(1, half)]
                                     + [ts0[half:] + ts1[0:1]], axis=0)
            cand_i = jnp.concatenate([ti0[0:1] * N_KEYS + ti1]
                                     + [ti0[i:i + 1] * N_KEYS + ti1[:half] for i in range(1, half)]
                                     + [ti0[half:] * N_KEYS + ti1[0:1]], axis=0)
            best_s, best_i = _topk_rows(cand_s, cand_i, PEER_TOPK)
            e = jnp.exp(best_s - best_s[0:1])
            g_parts.append(e / jnp.sum(e, axis=0, keepdims=True))
            idx_parts.append(best_i)
        idx_ref[rows, :] = jnp.concatenate(idx_parts, axis=0).T
        g_ref[rows, :] = jnp.concatenate(g_parts, axis=0).T


def _route(hn, w_pq_bf, sk_bf, tm):
    n = hn.shape[0]
    width = PEER_HEADS * PEER_TOPK
    return pl.pallas_call(
        functools.partial(_route_kernel, tm=tm),
        grid=(n // tm,),
        in_specs=[pl.BlockSpec((tm, D_MODEL), lambda i: (i, 0)),
                  _resident(w_pq_bf.shape), _resident(sk_bf.shape)],
        out_specs=[pl.BlockSpec((tm, width), lambda i: (i, 0)), pl.BlockSpec((tm, width), lambda i: (i, 0))],
        out_shape=[jax.ShapeDtypeStruct((n, width), jnp.int32), jax.ShapeDtypeStruct((n, width), _F32)],
        compiler_params=_params("parallel"),
        name="peer_route",
    )(hn, w_pq_bf, sk_bf)


SC_CORES = 2
SC_SUBCORES = 16
SC_LANES = 16
SC_TOKENS = 8
SC_RING = 4
GELU_C = math.sqrt(2.0 / math.pi)


def _pack_experts(expert_u, expert_v):
    def words(t):
        b = lax.bitcast_convert_type(t.astype(_BF16), jnp.uint16).astype(jnp.uint32)
        return b[:, :HALF] | (b[:, HALF:] << 16)
    return jnp.concatenate([words(expert_u), words(expert_v)], axis=1)


def _retrieve_body(idx_hbm, g_hbm, xp_hbm, h_hbm, tab_hbm, y_hbm,
                   idx_v, g_v, xp_v, o_v, rows_v, tr_v, in_sems, out_sems, row_sems, *, nblk, tb):
    lanes = SC_LANES
    wid = lax.axis_index("s") * SC_CORES + lax.axis_index("c")
    lane = lax.iota(jnp.int32, lanes)
    chunks = tb * PEER_HEADS
    mask_hi = jnp.uint32(0xFFFF0000)

    def halves(words):
        return plsc.bitcast(words << 16, _F32), plsc.bitcast(words & mask_hi, _F32)

    def stage_in(b, bs):
        t0 = (wid * nblk + b) * tb
        return [pltpu.make_async_copy(idx_hbm.at[pl.ds(t0 * PEER_HEADS, chunks)], idx_v.at[bs], in_sems.at[bs]),
                pltpu.make_async_copy(g_hbm.at[pl.ds(t0 * PEER_HEADS, chunks)], g_v.at[bs], in_sems.at[bs]),
                pltpu.make_async_copy(xp_hbm.at[pl.ds(t0, tb)], xp_v.at[bs], in_sems.at[bs]),
                pltpu.make_async_copy(h_hbm.at[pl.ds(t0, tb)], o_v.at[bs], in_sems.at[bs])]

    def stage_out(b, bs):
        t0 = (wid * nblk + b) * tb
        return pltpu.make_async_copy(o_v.at[bs], y_hbm.at[pl.ds(t0, tb)], out_sems.at[bs])

    def gather(bs, q, slot):
        return pltpu.make_async_copy(tab_hbm.at[idx_v.at[bs, q]], rows_v.at[slot], row_sems.at[slot])

    def head_chunk(bs, q, slot):
        tok = q // PEER_HEADS

        def dot_step(kc, accs):
            off1 = pl.multiple_of(kc * (2 * lanes), 2 * lanes)
            off2 = off1 + lanes
            x1 = plsc.bitcast(xp_v[bs, tok, pl.ds(off1, lanes)], _BF16)
            x2 = plsc.bitcast(xp_v[bs, tok, pl.ds(off2, lanes)], _BF16)
            out = []
            for r in range(PEER_TOPK):
                u1 = plsc.bitcast(rows_v[slot, r, pl.ds(off1, lanes)], _BF16)
                u2 = plsc.bitcast(rows_v[slot, r, pl.ds(off2, lanes)], _BF16)
                lo, hi = halves(plsc.bitcast(u1 * x1 + u2 * x2, jnp.uint32))
                out.append(accs[r] + (lo + hi))
            return tuple(out)

        zeros = tuple(jnp.zeros((lanes,), _F32) for _ in range(PEER_TOPK))
        accs = plsc.parallel_loop(0, HALF // (2 * lanes), carry=zeros)(dot_step)
        for r in range(PEER_TOPK):
            tr_v[r, :] = accs[r]
        d = jnp.zeros((lanes,), _F32)
        for c in range(lanes):
            d = d + plsc.load_gather(tr_v, [lane, jnp.full((lanes,), c, jnp.int32)])
        z = GELU_C * (d + 0.044715 * d * d * d)
        wgt = g_v[bs, q, :] * (0.5 * d * (2.0 - 2.0 / (jnp.exp(2.0 * z) + 1.0)))
        pairs = []
        for r in range(PEER_TOPK):
            sp = wgt.at[jnp.full((lanes,), r, jnp.int32)].get(mode="promise_in_bounds")
            pairs.append(plsc.pack(sp, sp, format=plsc.PackFormat.INTERLEAVED))

        def acc_step(kc):
            off = pl.multiple_of(kc * lanes, lanes)
            a_lo = jnp.zeros((lanes,), _F32)
            a_hi = jnp.zeros((lanes,), _F32)
            for r in range(0, PEER_TOPK, 2):
                v1 = plsc.bitcast(rows_v[slot, r, pl.ds(off + HALF, lanes)], _BF16)
                v2 = plsc.bitcast(rows_v[slot, r + 1, pl.ds(off + HALF, lanes)], _BF16)
                lo, hi = halves(plsc.bitcast(v1 * pairs[r] + v2 * pairs[r + 1], jnp.uint32))
                a_lo = a_lo + lo
                a_hi = a_hi + hi
            plsc.addupdate(o_v.at[bs, tok, pl.ds(off, lanes)], a_lo)
            plsc.addupdate(o_v.at[bs, tok, pl.ds(off + HALF, lanes)], a_hi)

        plsc.parallel_loop(0, HALF // lanes)(acc_step)

    for c in stage_in(0, 0):
        c.start()

    def block(b, _):
        bs = b % 2
        for c in stage_in(b, bs):
            c.wait()

        @pl.when(b + 1 < nblk)
        def _():
            @pl.when(b >= 1)
            def _():
                stage_out(b - 1, 1 - bs).wait()
            for c in stage_in(b + 1, 1 - bs):
                c.start()

        for s in range(SC_RING - 1):
            gather(bs, s, s).start()

        def chunk(q, _):
            nxt = q + SC_RING - 1

            @pl.when(nxt < chunks)
            def _():
                gather(bs, nxt, nxt % SC_RING).start()

            gather(bs, q, q % SC_RING).wait()
            head_chunk(bs, q, q % SC_RING)
            return 0

        lax.fori_loop(0, chunks, chunk, 0)
        stage_out(b, bs).start()
        return 0

    lax.fori_loop(0, nblk, block, 0)
    if nblk >= 2:
        stage_out(nblk - 2, (nblk - 2) % 2).wait()
    stage_out(nblk - 1, (nblk - 1) % 2).wait()


def _retrieve(idx, table, g, xp, h):
    n = h.shape[0]
    workers = SC_CORES * SC_SUBCORES
    tb = SC_TOKENS
    assert n % (workers * tb) == 0 and tb * PEER_HEADS >= SC_RING, n
    mesh = plsc.VectorSubcoreMesh(core_axis_name="c", subcore_axis_name="s")
    chunks = tb * PEER_HEADS
    call = pl.kernel(
        functools.partial(_retrieve_body, nblk=n // (workers * tb), tb=tb),
        out_type=jax.ShapeDtypeStruct((n, D_MODEL), _F32),
        mesh=mesh,
        scratch_types=[pltpu.VMEM((2, chunks, PEER_TOPK), jnp.int32),
                       pltpu.VMEM((2, chunks, PEER_TOPK), _F32),
                       pltpu.VMEM((2, tb, HALF), jnp.uint32),
                       pltpu.VMEM((2, tb, D_MODEL), _F32),
                       pltpu.VMEM((SC_RING, PEER_TOPK, D_MODEL), jnp.uint32),
                       pltpu.VMEM((PEER_TOPK, SC_LANES), _F32),
                       pltpu.SemaphoreType.DMA((2,)),
                       pltpu.SemaphoreType.DMA((2,)),
                       pltpu.SemaphoreType.DMA((SC_RING,))],
        compiler_params=pltpu.CompilerParams(needs_layout_passes=False),
        name="peer_retrieve",
    )
    return call(idx.reshape(n * PEER_HEADS, PEER_TOPK), g.reshape(n * PEER_HEADS, PEER_TOPK), xp, h, table)


def _row_tile(n):
    return min(256, n)


PROMPT_GROUPS = 4


def _layer(x, cache_k, cache_v, hist, lambda_init, wts, groups):
    per = x.shape[0] // groups
    outs = []
    for gi in range(groups):
        sl = slice(gi * per, (gi + 1) * per)
        ck = None if cache_k is None else cache_k[sl]
        cv = None if cache_v is None else cache_v[sl]
        outs.append(_layer_group(x[sl], ck, cv, hist[sl], lambda_init, wts))
    if groups == 1:
        return outs[0]
    return tuple(jnp.concatenate(parts, axis=0) for parts in zip(*outs))


def _layer_group(x, cache_k, cache_v, hist, lambda_init, wts):
    batch, seq, _ = x.shape
    n = batch * seq
    xf = x.reshape(n, D_MODEL)
    tm = _row_tile(n)
    prompt = cache_k is None
    qs, kf, kb, vf, vb, u, gt = _inproj(xf, wts["g_mix"], wts["w_in"], wts["b_gate"], wts["gq"], wts["gk"], tm,
                                        v_transposed=prompt)
    if prompt:
        assert seq % tm == 0, (seq, tm)
        o = _prompt_attention(qs, kb, vb, wts["lam4"], wts["g_sub"].reshape(V_DIM, 1), batch, seq, tm,
                              lambda_init)
    else:
        o = _decode_attention(qs, kb, vb, cache_k, cache_v, wts["lam4"], wts["g_sub"], batch, seq, lambda_init)
    u3 = u.reshape(batch, seq, D_MODEL)
    c = _conv_branch(u3, hist, wts["w_dw"], wts["b_dw"], wts["g_cln"], wts["b_cln"], min(256, seq))
    h, hn, xp = _merge(o, c.reshape(n, D_MODEL), gt, xf, wts["w_attn_out"], wts["w_conv_out"], wts["w_out"],
                       wts["g_ffn"], tm)
    idx, g = _route(hn, wts["w_pq"], wts["sub_keys"], tm)
    y = _retrieve(idx, wts["table"], g, xp, h)
    keep = CONV_K - 1
    if seq >= keep:
        conv_state = u3[:, seq - keep:]
    else:
        conv_state = jnp.concatenate([hist[:, HIST_ROWS - keep + seq:], u3], axis=1)
    return (y.reshape(batch, seq, D_MODEL), kf.reshape(batch, seq, N_HEADS, 2, HEAD_DIM),
            vf.reshape(batch, seq, N_HEADS, V_DIM), conv_state)


def _layer_weights(l, g_mix, w_in, b_gate, g_qn, g_kn, lam_q1, lam_k1, lam_q2, lam_k2, g_sub,
                   w_attn_out, w_dw, b_dw, g_cln, b_cln, w_conv_out, w_out, g_ffn, w_pq, sub_keys,
                   expert_u, expert_v):
    row = lambda a: a[l].reshape(1, -1)
    return dict(
        g_mix=row(g_mix), w_in=w_in[l].astype(_BF16), b_gate=row(b_gate),
        gq=jnp.tile(g_qn[l], 2).reshape(1, LANES), gk=jnp.tile(g_kn[l], 2).reshape(1, LANES),
        lam4=jnp.stack([lam_q1[l], lam_k1[l], lam_q2[l], lam_k2[l]]), g_sub=row(g_sub),
        w_attn_out=w_attn_out[l].astype(_BF16),
        w_dw=jnp.pad(w_dw[l].reshape(CONV_K, D_MODEL), ((0, HIST_ROWS - CONV_K), (0, 0))),
        b_dw=row(b_dw), g_cln=row(g_cln), b_cln=row(b_cln),
        w_conv_out=w_conv_out[l].astype(_BF16), w_out=w_out[l].astype(_BF16), g_ffn=row(g_ffn),
        w_pq=w_pq[l].astype(_BF16),
        sub_keys=sub_keys[l].reshape(2 * PEER_HEADS, N_KEYS, -1).astype(_BF16),
        table=_pack_experts(expert_u[l], expert_v[l]),
    )


def kernel(x_prompt, x_sample, cache_k, cache_v, cache_conv, g_mix, w_in, b_gate, g_qn, g_kn, lam_q1, lam_k1, lam_q2, lam_k2, g_sub, w_attn_out, w_dw, b_dw, g_cln, b_cln, w_conv_out, w_out, g_ffn, w_pq, sub_keys, expert_u, expert_v):
    depth = w_in.shape[0]
    yp, ys = x_prompt, x_sample
    pad_hist = lambda hst: jnp.pad(hst, ((0, 0), (HIST_ROWS - (CONV_K - 1), 0), (0, 0)))
    zero_hist = jnp.zeros((x_prompt.shape[0], HIST_ROWS, D_MODEL), x_prompt.dtype)
    outs = [[] for _ in range(6)]
    for l in range(depth):
        lambda_init = 0.8 - 0.6 * math.exp(-0.3 * l)
        wts = _layer_weights(l, g_mix, w_in, b_gate, g_qn, g_kn, lam_q1, lam_k1, lam_q2, lam_k2, g_sub,
                             w_attn_out, w_dw, b_dw, g_cln, b_cln, w_conv_out, w_out, g_ffn, w_pq,
                             sub_keys, expert_u, expert_v)
        yp, kp, vp, cp = _layer(yp, None, None, zero_hist, lambda_init, wts,
                                groups=math.gcd(PROMPT_GROUPS, yp.shape[0]))
        db, past = cache_k.shape[1], cache_k.shape[2]
        ys, kn, vn, cn = _layer(ys, cache_k[l].reshape(db, past, D_MODEL), cache_v[l].reshape(db, past, D_MODEL),
                                pad_hist(cache_conv[l]), lambda_init, wts, groups=1)
        for lst, val in zip(outs, (kp, vp, cp, kn, vn, cn)):
            lst.append(val)
    return (yp, ys) + tuple(jnp.stack(lst) for lst in outs)
```

```python
import functools
import math

import jax
import jax.numpy as jnp
from jax import lax
from jax.experimental import pallas as pl
from jax.experimental.pallas import tpu as pltpu
from jax.experimental.pallas import tpu_sc as plsc

D_MODEL = 1024
CHUNK = 64
N_HEADS = 8
HEAD_DIM = 64
V_DIM = 2 * HEAD_DIM
ATTN_SCALE = HEAD_DIM ** -0.5
CONV_K = 31
PEER_HEADS = 8
N_KEYS = 128
PEER_TOPK = 16
EPS = 1e-6
NEG_INF = -1e30
HALF = D_MODEL // 2

LANES = 128
HIST_ROWS = 32
VMEM_LIMIT = 48 * 1024 * 1024

_BF16 = jnp.bfloat16
_F32 = jnp.float32


def _resident(shape):
    nd = len(shape)
    return pl.BlockSpec(shape, lambda *_: (0,) * nd, pipeline_mode=pl.Buffered(1))


def _params(*sem):
    return pltpu.CompilerParams(dimension_semantics=sem, vmem_limit_bytes=VMEM_LIMIT)


def _inproj_kernel(x_ref, gmix_ref, w_ref, bg_ref, gq_ref, gk_ref,
                   qs_ref, kf_ref, kb_ref, vf_ref, vb_ref, u_ref, gt_ref, *, v_transposed):
    x = x_ref[...]
    ms = jnp.mean(x * x, axis=-1, keepdims=True)
    xn = (x * lax.rsqrt(ms + EPS) * gmix_ref[...]).astype(_BF16)

    def proj(c):
        return jnp.dot(xn, w_ref[:, c * D_MODEL:(c + 1) * D_MODEL], preferred_element_type=_F32)

    lo = lax.broadcasted_iota(jnp.int32, (1, LANES), 1) < HEAD_DIM

    def head_norm(zh, g):
        sq = zh * zh
        ss_lo = jnp.sum(jnp.where(lo, sq, 0.0), axis=-1, keepdims=True)
        ss_hi = jnp.sum(jnp.where(lo, 0.0, sq), axis=-1, keepdims=True)
        r = jnp.where(lo, lax.rsqrt(ss_lo / HEAD_DIM + EPS), lax.rsqrt(ss_hi / HEAD_DIM + EPS))
        return zh * r * g

    zq = proj(0)
    for h in range(N_HEADS):
        sl = slice(h * LANES, (h + 1) * LANES)
        qn = head_norm(zq[:, sl], gq_ref[...]) * ATTN_SCALE
        qs_ref[0, :, sl] = jnp.where(lo, qn, 0.0).astype(_BF16)
        qs_ref[1, :, sl] = jnp.where(lo, 0.0, qn).astype(_BF16)
    zk = proj(1)
    for h in range(N_HEADS):
        sl = slice(h * LANES, (h + 1) * LANES)
        kn = head_norm(zk[:, sl], gk_ref[...])
        kf_ref[:, sl] = kn
        kb_ref[:, sl] = kn.astype(_BF16)
    zv = proj(2)
    vf_ref[...] = zv
    if v_transposed:
        vb_ref[0] = zv.T.astype(_BF16)
    else:
        vb_ref[...] = zv.astype(_BF16)
    u_ref[...] = proj(3) * jax.nn.sigmoid(proj(4))
    gt_ref[:, :D_MODEL] = jax.nn.sigmoid(proj(5) + bg_ref[:, :D_MODEL])
    gt_ref[:, D_MODEL:] = jax.nn.sigmoid(proj(6) + bg_ref[:, D_MODEL:])


def _inproj(x, row0, n, g_mix, w_in_bf, b_gate, gq, gk, tm, v_transposed):
    assert row0 % tm == 0 and n % tm == 0, (row0, n, tm)
    row = lambda w: pl.BlockSpec((tm, w), lambda i: (i, 0))
    x_spec = pl.BlockSpec((tm, D_MODEL), lambda i: (i + row0 // tm, 0))
    if v_transposed:
        vb_spec = pl.BlockSpec((1, D_MODEL, tm), lambda i: (i, 0, 0))
        vb_shape = jax.ShapeDtypeStruct((n // tm, D_MODEL, tm), _BF16)
    else:
        vb_spec, vb_shape = row(D_MODEL), jax.ShapeDtypeStruct((n, D_MODEL), _BF16)
    return pl.pallas_call(
        functools.partial(_inproj_kernel, v_transposed=v_transposed),
        grid=(n // tm,),
        in_specs=[x_spec, _resident((1, D_MODEL)), _resident(w_in_bf.shape),
                  _resident((1, 2 * D_MODEL)), _resident((1, LANES)), _resident((1, LANES))],
        out_specs=[pl.BlockSpec((2, tm, D_MODEL), lambda i: (0, i, 0)),
                   row(D_MODEL), row(D_MODEL), row(D_MODEL), vb_spec, row(D_MODEL),
                   row(2 * D_MODEL)],
        out_shape=[jax.ShapeDtypeStruct((2, n, D_MODEL), _BF16),
                   jax.ShapeDtypeStruct((n, D_MODEL), _F32),
                   jax.ShapeDtypeStruct((n, D_MODEL), _BF16),
                   jax.ShapeDtypeStruct((n, D_MODEL), _F32),
                   vb_shape,
                   jax.ShapeDtypeStruct((n, D_MODEL), _F32),
                   jax.ShapeDtypeStruct((n, 2 * D_MODEL), _F32)],
        compiler_params=_params("parallel"),
        name="inproj",
    )(x, g_mix, w_in_bf, b_gate, gq, gk)


def _lambda(lam_ref, lambda_init):
    lam = lam_ref[...]
    a = jnp.sum(lam[0:1] * lam[1:2], axis=-1, keepdims=True)
    b = jnp.sum(lam[2:3] * lam[3:4], axis=-1, keepdims=True)
    return jnp.exp(a) - jnp.exp(b) + lambda_init


def _attn_finish(acc, l, tq, lam, gsub_ref, lambda_init):
    o = acc / l
    o = o[:tq] - lam * o[tq:]
    o = o * lax.rsqrt(jnp.mean(o * o, axis=-1, keepdims=True) + EPS) * gsub_ref[...]
    return (o * (1.0 - lambda_init)).astype(_BF16)


def _qk(q, k):
    return lax.dot_general(q, k, (((1,), (1,)), ((), ())), preferred_element_type=_F32)


ATTN_HEADS_PER_STEP = 2
ATTN_Q_TILE = 1024


def _prompt_attn_kernel(q_ref, k_ref, vt_ref, lam_ref, gsub_ref, o_ref, *, tq, tk, lambda_init):
    qi = pl.program_id(2)
    heads = range(ATTN_HEADS_PER_STEP)
    hs = lambda h: slice(h * LANES, (h + 1) * LANES)
    q = [jnp.concatenate([q_ref[0, :, hs(h)], q_ref[1, :, hs(h)]], axis=0) for h in heads]

    def step(j, carry, masked):
        off = pl.multiple_of(j * tk, tk)
        out = []
        for h in heads:
            m, l, acc = carry[h]
            s = _qk(k_ref[pl.ds(off, tk), hs(h)], q[h])
            if masked:
                kc = (off + lax.broadcasted_iota(jnp.int32, s.shape, 0)) // CHUNK
                qc = (qi * tq + lax.broadcasted_iota(jnp.int32, s.shape, 1) % tq) // CHUNK
                s = jnp.where(kc <= qc, s, NEG_INF)
            m_new = jnp.maximum(m, jnp.max(s, axis=0, keepdims=True))
            alpha = jnp.exp(m - m_new)
            p = jnp.exp(s - m_new)
            l = alpha * l + jnp.sum(p, axis=0, keepdims=True)
            acc = alpha * acc + jnp.dot(vt_ref[j, hs(h), :], p.astype(_BF16), preferred_element_type=_F32)
            out.append((m_new, l, acc))
        return tuple(out)

    init = tuple((jnp.full((1, 2 * tq), NEG_INF, _F32), jnp.zeros((1, 2 * tq), _F32),
                  jnp.zeros((V_DIM, 2 * tq), _F32)) for _ in heads)
    ratio = tq // tk
    carry = lax.fori_loop(0, qi * ratio, lambda j, c: step(j, c, False), init)
    for d in range(ratio):
        carry = step(qi * ratio + d, carry, True)
    lam = _lambda(lam_ref, lambda_init)
    for h in heads:
        _, l, acc = carry[h]
        o = acc / l
        o = o[:, :tq] - lam * o[:, tq:]
        o = o * lax.rsqrt(jnp.mean(o * o, axis=0, keepdims=True) + EPS) * gsub_ref[...]
        o_ref[:, hs(h)] = (o * (1.0 - lambda_init)).T.astype(_BF16)


def _prompt_attention(qs, kb, vt, lam4, gsub_col, batch, seq, tk, lambda_init):
    n = batch * seq
    tq = ATTN_Q_TILE if seq % ATTN_Q_TILE == 0 and ATTN_Q_TILE % tk == 0 else tk
    nq = seq // tq
    hw = ATTN_HEADS_PER_STEP * LANES
    return pl.pallas_call(
        functools.partial(_prompt_attn_kernel, tq=tq, tk=tk, lambda_init=lambda_init),
        grid=(batch, N_HEADS // ATTN_HEADS_PER_STEP, nq),
        in_specs=[pl.BlockSpec((2, tq, hw), lambda b, h, i: (0, b * nq + i, h)),
                  pl.BlockSpec((seq, hw), lambda b, h, i: (b, h)),
                  pl.BlockSpec((seq // tk, hw, tk), lambda b, h, i: (b, h, 0)),
                  _resident((4, HEAD_DIM)), _resident((V_DIM, 1))],
        out_specs=pl.BlockSpec((tq, hw), lambda b, h, i: (b * nq + i, h)),
        out_shape=jax.ShapeDtypeStruct((n, D_MODEL), _BF16),
        compiler_params=_params("parallel", "parallel", "arbitrary"),
        name="prompt_attn",
    )(qs, kb, vt, lam4, gsub_col)


def _decode_attn_kernel(q_ref, kn_ref, vn_ref, kc_ref, vc_ref, lam_ref, gsub_ref, o_ref, *, tq, lambda_init):
    q = jnp.concatenate([q_ref[0], q_ref[1]], axis=0)
    s_c = _qk(q, kc_ref[0].astype(_BF16))
    s_n = _qk(q, kn_ref[...])
    m = jnp.maximum(jnp.max(s_c, axis=-1, keepdims=True), jnp.max(s_n, axis=-1, keepdims=True))
    p_c = jnp.exp(s_c - m)
    p_n = jnp.exp(s_n - m)
    l = jnp.sum(p_c, axis=-1, keepdims=True) + jnp.sum(p_n, axis=-1, keepdims=True)
    acc = (jnp.dot(p_c.astype(_BF16), vc_ref[0].astype(_BF16), preferred_element_type=_F32)
           + jnp.dot(p_n.astype(_BF16), vn_ref[...], preferred_element_type=_F32))
    o_ref[...] = _attn_finish(acc, l, tq, _lambda(lam_ref, lambda_init), gsub_ref, lambda_init)


def _decode_attention(qs, kb, vb, cache_k, cache_v, lam4, gsub, batch, seq, lambda_init):
    past = cache_k.shape[1]
    return pl.pallas_call(
        functools.partial(_decode_attn_kernel, tq=seq, lambda_init=lambda_init),
        grid=(batch, N_HEADS),
        in_specs=[pl.BlockSpec((2, seq, LANES), lambda b, h: (0, b, h)),
                  pl.BlockSpec((seq, LANES), lambda b, h: (b, h)),
                  pl.BlockSpec((seq, LANES), lambda b, h: (b, h)),
                  pl.BlockSpec((1, past, LANES), lambda b, h: (b, 0, h)),
                  pl.BlockSpec((1, past, LANES), lambda b, h: (b, 0, h)),
                  _resident((4, HEAD_DIM)), _resident((1, V_DIM))],
        out_specs=pl.BlockSpec((seq, LANES), lambda b, h: (b, h)),
        out_shape=jax.ShapeDtypeStruct((batch * seq, D_MODEL), _BF16),
        compiler_params=_params("parallel", "parallel"),
        name="decode_attn",
    )(qs, kb, vb, cache_k, cache_v, lam4, gsub)


CONV_ROWS = 32


def _conv_kernel(u_ref, hist_ref, w_ref, b_ref, g_ref, beta_ref, y_ref, win_ref, *, tt):
    @pl.when(pl.program_id(1) == 0)
    def _():
        win_ref[0:HIST_ROWS, :] = hist_ref[0]

    win_ref[HIST_ROWS:HIST_ROWS + tt, :] = u_ref[0]
    first = HIST_ROWS - (CONV_K - 1)
    rows = min(CONV_ROWS, tt)
    for r0 in range(0, tt, rows):
        acc = jnp.broadcast_to(b_ref[...], (rows, D_MODEL))
        for j in range(CONV_K):
            acc = acc + w_ref[j:j + 1, :] * win_ref[first + r0 + j:first + r0 + j + rows, :]
        mu = jnp.mean(acc, axis=-1, keepdims=True)
        xc = acc - mu
        y = xc * lax.rsqrt(jnp.mean(xc * xc, axis=-1, keepdims=True) + EPS)
        y = y * g_ref[...] + beta_ref[...]
        y_ref[0, r0:r0 + rows, :] = (y * jax.nn.sigmoid(y)).astype(_BF16)
    win_ref[0:HIST_ROWS, :] = win_ref[tt:tt + HIST_ROWS, :]


def _conv_branch(u, hist, w_dw, b_dw, g_cln, b_cln, tt):
    batch, seq, _ = u.shape
    return pl.pallas_call(
        functools.partial(_conv_kernel, tt=tt),
        grid=(batch, seq // tt),
        in_specs=[pl.BlockSpec((1, tt, D_MODEL), lambda b, i: (b, i, 0)),
                  pl.BlockSpec((1, HIST_ROWS, D_MODEL), lambda b, i: (b, 0, 0)),
                  _resident((HIST_ROWS, D_MODEL)), _resident((1, D_MODEL)),
                  _resident((1, D_MODEL)), _resident((1, D_MODEL))],
        out_specs=pl.BlockSpec((1, tt, D_MODEL), lambda b, i: (b, i, 0)),
        out_shape=jax.ShapeDtypeStruct((batch, seq, D_MODEL), _BF16),
        scratch_shapes=[pltpu.VMEM((HIST_ROWS + max(tt, HIST_ROWS), D_MODEL), _F32)],
        compiler_params=_params("parallel", "arbitrary"),
        name="conv_branch",
    )(u, hist, w_dw, b_dw, g_cln, b_cln)


def _merge_kernel(o_ref, c_ref, gt_ref, x_ref, wa_ref, wc_ref, wo_ref, gffn_ref, h_ref, hn_ref, xp_ref):
    attn = jnp.dot(o_ref[...], wa_ref[...], preferred_element_type=_F32)
    conv = jnp.dot(c_ref[...], wc_ref[...], preferred_element_type=_F32)
    mix = gt_ref[:, :D_MODEL] * attn + gt_ref[:, D_MODEL:] * conv
    h = x_ref[...] + jnp.dot(mix.astype(_BF16), wo_ref[...], preferred_element_type=_F32)
    h_ref[...] = h
    hn = h * lax.rsqrt(jnp.mean(h * h, axis=-1, keepdims=True) + EPS) * gffn_ref[...]
    hn_bf = hn.astype(_BF16)
    hn_ref[...] = hn_bf
    bits = pltpu.bitcast(hn_bf.astype(_F32), jnp.uint32)
    xp_ref[...] = (bits[:, :HALF] >> 16) | (bits[:, HALF:] & jnp.uint32(0xFFFF0000))


def _merge(o, c, gt, x, row0, wa, wc, wo, g_ffn, tm):
    n = o.shape[0]
    row = lambda w: pl.BlockSpec((tm, w), lambda i: (i, 0))
    x_spec = pl.BlockSpec((tm, D_MODEL), lambda i: (i + row0 // tm, 0))
    sq = (D_MODEL, D_MODEL)
    return pl.pallas_call(
        _merge_kernel,
        grid=(n // tm,),
        in_specs=[row(D_MODEL), row(D_MODEL), row(2 * D_MODEL), x_spec,
                  _resident(sq), _resident(sq), _resident(sq), _resident((1, D_MODEL))],
        out_specs=[row(D_MODEL), row(D_MODEL), row(HALF)],
        out_shape=[jax.ShapeDtypeStruct((n, D_MODEL), _F32), jax.ShapeDtypeStruct((n, D_MODEL), _BF16),
                   jax.ShapeDtypeStruct((n, HALF), jnp.uint32)],
        compiler_params=_params("parallel"),
        name="merge",
    )(o, c, gt, x, wa, wc, wo, g_ffn)


_KEY_SENTINEL = N_KEYS * N_KEYS


def _topk_rows(s, keys, k):
    vals, sels = [], []
    for r in range(k):
        m = jnp.max(s, axis=0, keepdims=True)
        sel = jnp.min(jnp.where(s == m, keys, _KEY_SENTINEL), axis=0, keepdims=True)
        vals.append(m)
        sels.append(sel)
        if r + 1 < k:
            s = jnp.where(keys == sel, -jnp.inf, s)
    return jnp.concatenate(vals, axis=0), jnp.concatenate(sels, axis=0)


def _route_kernel(hn_ref, wpq_ref, sk_ref, idx_ref, g_ref, *, tm):
    q = jnp.dot(hn_ref[...].astype(_BF16), wpq_ref[...], preferred_element_type=_F32).astype(_BF16)
    key_iota = lax.broadcasted_iota(jnp.int32, (N_KEYS, LANES), 0)
    for lt in range(tm // LANES):
        rows = slice(lt * LANES, (lt + 1) * LANES)
        idx_parts, g_parts = [], []
        for h in range(PEER_HEADS):
            top = []
            for p in range(2):
                hp = 2 * h + p
                st = _qk(sk_ref[hp], q[rows, hp * LANES:(hp + 1) * LANES])
                top.append(_topk_rows(st, key_iota, PEER_TOPK))
            (ts0, ti0), (ts1, ti1) = top
            half = PEER_TOPK // 2
            cand_s = jnp.concatenate([ts0[0:1] + ts1]
                                     + [ts0[i:i + 1] + ts1[:half] for i in range(1, half)]
                                     + [ts0[half:] + ts1[0:1]], axis=0)
            cand_i = jnp.concatenate([ti0[0:1] * N_KEYS + ti1]
                                     + [ti0[i:i + 1] * N_KEYS + ti1[:half] for i in range(1, half)]
                                     + [ti0[half:] * N_KEYS + ti1[0:1]], axis=0)
            best_s, best_i = _topk_rows(cand_s, cand_i, PEER_TOPK)
            e = jnp.exp(best_s - best_s[0:1])
            g_parts.append(e / jnp.sum(e, axis=0, keepdims=True))
            idx_parts.append(best_i)
        idx_ref[rows, :] = jnp.concatenate(idx_parts, axis=0).T
        g_ref[rows, :] = jnp.concatenate(g_parts, axis=0).T


def _route(hn, w_pq_bf, sk_bf, tm):
    n = hn.shape[0]
    width = PEER_HEADS * PEER_TOPK
    return pl.pallas_call(
        functools.partial(_route_kernel, tm=tm),
        grid=(n // tm,),
        in_specs=[pl.BlockSpec((tm, D_MODEL), lambda i: (i, 0)),
                  _resident(w_pq_bf.shape), _resident(sk_bf.shape)],
        out_specs=[pl.BlockSpec((tm, width), lambda i: (i, 0)), pl.BlockSpec((tm, width), lambda i: (i, 0))],
        out_shape=[jax.ShapeDtypeStruct((n, width), jnp.int32), jax.ShapeDtypeStruct((n, width), _F32)],
        compiler_params=_params("parallel"),
        name="peer_route",
    )(hn, w_pq_bf, sk_bf)


SC_CORES = 2
SC_SUBCORES = 16
SC_LANES = 16
SC_TOKENS = 8
SC_RING = 4
SC_BF16_TERMS = 4
GELU_C = math.sqrt(2.0 / math.pi)


def _pack_experts(expert_u, expert_v):
    def words(t):
        b = lax.bitcast_convert_type(t.astype(_BF16), jnp.uint16).astype(jnp.uint32)
        return b[:, :HALF] | (b[:, HALF:] << 16)
    return jnp.concatenate([words(expert_u), words(expert_v)], axis=1)


def _retrieve_body(idx_hbm, g_hbm, xp_hbm, h_hbm, tab_hbm, y_hbm,
                   idx_v, g_v, xp_v, o_v, rows_v, tr_v, in_sems, out_sems, row_sems, *, nblk, tb):
    lanes = SC_LANES
    wid = lax.axis_index("s") * SC_CORES + lax.axis_index("c")
    lane = lax.iota(jnp.int32, lanes)
    chunks = tb * PEER_HEADS
    mask_hi = jnp.uint32(0xFFFF0000)

    def halves(words):
        return plsc.bitcast(words << 16, _F32), plsc.bitcast(words & mask_hi, _F32)

    def bf16_dot(a, b):
        return (a[0] * b[0] + a[1] * b[1]) + (a[2] * b[2] + a[3] * b[3])

    def stage_in(b, bs):
        t0 = (wid * nblk + b) * tb
        return [pltpu.make_async_copy(idx_hbm.at[pl.ds(t0 * PEER_HEADS, chunks)], idx_v.at[bs], in_sems.at[bs]),
                pltpu.make_async_copy(g_hbm.at[pl.ds(t0 * PEER_HEADS, chunks)], g_v.at[bs], in_sems.at[bs]),
                pltpu.make_async_copy(xp_hbm.at[pl.ds(t0, tb)], xp_v.at[bs], in_sems.at[bs]),
                pltpu.make_async_copy(h_hbm.at[pl.ds(t0, tb)], o_v.at[bs], in_sems.at[bs])]

    def stage_out(b, bs):
        t0 = (wid * nblk + b) * tb
        return pltpu.make_async_copy(o_v.at[bs], y_hbm.at[pl.ds(t0, tb)], out_sems.at[bs])

    def gather(bs, q, slot):
        return pltpu.make_async_copy(tab_hbm.at[idx_v.at[bs, q]], rows_v.at[slot], row_sems.at[slot])

    def head_chunk(bs, q, slot):
        tok = q // PEER_HEADS

        def dot_step(kc, accs):
            offs = [pl.multiple_of(kc * (SC_BF16_TERMS * lanes), SC_BF16_TERMS * lanes) + i * lanes
                    for i in range(SC_BF16_TERMS)]
            xs = [plsc.bitcast(xp_v[bs, tok, pl.ds(o, lanes)], _BF16) for o in offs]
            out = []
            for r in range(PEER_TOPK):
                us = [plsc.bitcast(rows_v[slot, r, pl.ds(o, lanes)], _BF16) for o in offs]
                lo, hi = halves(plsc.bitcast(bf16_dot(us, xs), jnp.uint32))
                out.append(accs[r] + (lo + hi))
            return tuple(out)

        zeros = tuple(jnp.zeros((lanes,), _F32) for _ in range(PEER_TOPK))
        accs = plsc.parallel_loop(0, HALF // (SC_BF16_TERMS * lanes), carry=zeros)(dot_step)
        for r in range(PEER_TOPK):
            tr_v[r, :] = accs[r]
        d = jnp.zeros((lanes,), _F32)
        for c in range(lanes):
            d = d + plsc.load_gather(tr_v, [lane, jnp.full((lanes,), c, jnp.int32)])
        z = GELU_C * (d + 0.044715 * d * d * d)
        wgt = g_v[bs, q, :] * (0.5 * d * (2.0 - 2.0 / (jnp.exp(2.0 * z) + 1.0)))
        pairs = []
        for r in range(PEER_TOPK):
            sp = wgt.at[jnp.full((lanes,), r, jnp.int32)].get(mode="promise_in_bounds")
            pairs.append(plsc.pack(sp, sp, format=plsc.PackFormat.INTERLEAVED))

        def acc_step(kc):
            off = pl.multiple_of(kc * lanes, lanes)
            a_lo = jnp.zeros((lanes,), _F32)
            a_hi = jnp.zeros((lanes,), _F32)
            for r in range(0, PEER_TOPK, SC_BF16_TERMS):
                vs = [plsc.bitcast(rows_v[slot, r + i, pl.ds(off + HALF, lanes)], _BF16)
                      for i in range(SC_BF16_TERMS)]
                lo, hi = halves(plsc.bitcast(bf16_dot(vs, pairs[r:r + SC_BF16_TERMS]), jnp.uint32))
                a_lo = a_lo + lo
                a_hi = a_hi + hi
            plsc.addupdate(o_v.at[bs, tok, pl.ds(off, lanes)], a_lo)
            plsc.addupdate(o_v.at[bs, tok, pl.ds(off + HALF, lanes)], a_hi)

        plsc.parallel_loop(0, HALF // lanes)(acc_step)

    for c in stage_in(0, 0):
        c.start()

    def block(b, _):
        bs = b % 2
        for c in stage_in(b, bs):
            c.wait()

        @pl.when(b + 1 < nblk)
        def _():
            @pl.when(b >= 1)
            def _():
                stage_out(b - 1, 1 - bs).wait()
            for c in stage_in(b + 1, 1 - bs):
                c.start()

        for s in range(SC_RING - 1):
            gather(bs, s, s).start()

        def chunk(q, _):
            nxt = q + SC_RING - 1

            @pl.when(nxt < chunks)
            def _():
                gather(bs, nxt, nxt % SC_RING).start()

            gather(bs, q, q % SC_RING).wait()
            head_chunk(bs, q, q % SC_RING)
            return 0

        lax.fori_loop(0, chunks, chunk, 0)
        stage_out(b, bs).start()
        return 0

    lax.fori_loop(0, nblk, block, 0)
    if nblk >= 2:
        stage_out(nblk - 2, (nblk - 2) % 2).wait()
    stage_out(nblk - 1, (nblk - 1) % 2).wait()


def _retrieve(idx, table, g, xp, h):
    n = h.shape[0]
    workers = SC_CORES * SC_SUBCORES
    tb = SC_TOKENS
    assert n % (workers * tb) == 0 and tb * PEER_HEADS >= SC_RING, n
    mesh = plsc.VectorSubcoreMesh(core_axis_name="c", subcore_axis_name="s")
    chunks = tb * PEER_HEADS
    call = pl.kernel(
        functools.partial(_retrieve_body, nblk=n // (workers * tb), tb=tb),
        out_type=jax.ShapeDtypeStruct((n, D_MODEL), _F32),
        mesh=mesh,
        scratch_types=[pltpu.VMEM((2, chunks, PEER_TOPK), jnp.int32),
                       pltpu.VMEM((2, chunks, PEER_TOPK), _F32),
                       pltpu.VMEM((2, tb, HALF), jnp.uint32),
                       pltpu.VMEM((2, tb, D_MODEL), _F32),
                       pltpu.VMEM((SC_RING, PEER_TOPK, D_MODEL), jnp.uint32),
                       pltpu.VMEM((PEER_TOPK, SC_LANES), _F32),
                       pltpu.SemaphoreType.DMA((2,)),
                       pltpu.SemaphoreType.DMA((2,)),
                       pltpu.SemaphoreType.DMA((SC_RING,))],
        compiler_params=pltpu.CompilerParams(needs_layout_passes=False),
        name="peer_retrieve",
    )
    return call(idx.reshape(n * PEER_HEADS, PEER_TOPK), g.reshape(n * PEER_HEADS, PEER_TOPK), xp, h, table)


def _row_tile(n):
    return min(256, n)


PROMPT_GROUPS = 4


def _layer(x_prompt, x_decode, cache_k, cache_v, hist_decode, lambda_init, wts):
    pb, seq, _ = x_prompt.shape
    groups = math.gcd(PROMPT_GROUPS, pb)
    per = pb // groups
    xp_flat = x_prompt.reshape(pb * seq, D_MODEL)
    zero_hist = jnp.zeros((per, HIST_ROWS, D_MODEL), x_prompt.dtype)
    prompt_outs = [_layer_group(xp_flat, gi * per * seq, per, seq, None, None, zero_hist, lambda_init, wts)
                   for gi in range(groups)]
    if groups > 1:
        prompt_outs = [tuple(jnp.concatenate(parts, axis=0) for parts in zip(*prompt_outs))]
    decode_outs = _layer_group(x_decode.reshape(-1, D_MODEL), 0, x_decode.shape[0], x_decode.shape[1],
                               cache_k, cache_v, hist_decode, lambda_init, wts)
    return prompt_outs[0], decode_outs


def _layer_group(xf, row0, batch, seq, cache_k, cache_v, hist, lambda_init, wts):
    g_mix = wts["g_mix"]
    n = batch * seq
    tm = _row_tile(n)
    prompt = cache_k is None
    qs, kf, kb, vf, vb, u, gt = _inproj(xf, row0, n, g_mix, wts["w_in"], wts["b_gate"], wts["gq"], wts["gk"], tm,
                                        v_transposed=prompt)
    if prompt:
        assert seq % tm == 0, (seq, tm)
        o = _prompt_attention(qs, kb, vb, wts["lam4"], wts["g_sub"].reshape(V_DIM, 1), batch, seq, tm,
                              lambda_init)
    else:
        o = _decode_attention(qs, kb, vb, cache_k, cache_v, wts["lam4"], wts["g_sub"], batch, seq, lambda_init)
    u3 = u.reshape(batch, seq, D_MODEL)
    c = _conv_branch(u3, hist, wts["w_dw"], wts["b_dw"], wts["g_cln"], wts["b_cln"], min(256, seq))
    h, hn, xp = _merge(o, c.reshape(n, D_MODEL), gt, xf, row0, wts["w_attn_out"], wts["w_conv_out"],
                       wts["w_out"], wts["g_ffn"], tm)
    idx, g = _route(hn, wts["w_pq"], wts["sub_keys"], tm)
    y = _retrieve(idx, wts["table"], g, xp, h)
    keep = CONV_K - 1
    if seq >= keep:
        conv_state = u3[:, seq - keep:]
    else:
        conv_state = jnp.concatenate([hist[:, HIST_ROWS - keep + seq:], u3], axis=1)
    return (y.reshape(batch, seq, D_MODEL), kf.reshape(batch, seq, N_HEADS, 2, HEAD_DIM),
            vf.reshape(batch, seq, N_HEADS, V_DIM), conv_state)


def _layer_weights(l, g_mix, w_in, b_gate, g_qn, g_kn, lam_q1, lam_k1, lam_q2, lam_k2, g_sub,
                   w_attn_out, w_dw, b_dw, g_cln, b_cln, w_conv_out, w_out, g_ffn, w_pq, sub_keys,
                   expert_u, expert_v):
    row = lambda a: a[l].reshape(1, -1)
    return dict(
        g_mix=row(g_mix), w_in=w_in[l].astype(_BF16), b_gate=row(b_gate),
        gq=jnp.tile(g_qn[l], 2).reshape(1, LANES), gk=jnp.tile(g_kn[l], 2).reshape(1, LANES),
        lam4=jnp.stack([lam_q1[l], lam_k1[l], lam_q2[l], lam_k2[l]]), g_sub=row(g_sub),
        w_attn_out=w_attn_out[l].astype(_BF16),
        w_dw=jnp.pad(w_dw[l].reshape(CONV_K, D_MODEL), ((0, HIST_ROWS - CONV_K), (0, 0))),
        b_dw=row(b_dw), g_cln=row(g_cln), b_cln=row(b_cln),
        w_conv_out=w_conv_out[l].astype(_BF16), w_out=w_out[l].astype(_BF16), g_ffn=row(g_ffn),
        w_pq=w_pq[l].astype(_BF16),
        sub_keys=sub_keys[l].reshape(2 * PEER_HEADS, N_KEYS, -1).astype(_BF16),
        table=_pack_experts(expert_u[l], expert_v[l]),
    )


def kernel(x_prompt, x_sample, cache_k, cache_v, cache_conv, g_mix, w_in, b_gate, g_qn, g_kn, lam_q1, lam_k1, lam_q2, lam_k2, g_sub, w_attn_out, w_dw, b_dw, g_cln, b_cln, w_conv_out, w_out, g_ffn, w_pq, sub_keys, expert_u, expert_v):
    depth = w_in.shape[0]
    yp, ys = x_prompt, x_sample
    pad_hist = lambda hst: jnp.pad(hst, ((0, 0), (HIST_ROWS - (CONV_K - 1), 0), (0, 0)))
    outs = [[] for _ in range(6)]
    for l in range(depth):
        lambda_init = 0.8 - 0.6 * math.exp(-0.3 * l)
        wts = _layer_weights(l, g_mix, w_in, b_gate, g_qn, g_kn, lam_q1, lam_k1, lam_q2, lam_k2, g_sub,
                             w_attn_out, w_dw, b_dw, g_cln, b_cln, w_conv_out, w_out, g_ffn, w_pq,
                             sub_keys, expert_u, expert_v)
        db, past = cache_k.shape[1], cache_k.shape[2]
        (yp, kp, vp, cp), (ys, kn, vn, cn) = _layer(
            yp, ys, cache_k[l].reshape(db, past, D_MODEL), cache_v[l].reshape(db, past, D_MODEL),
            pad_hist(cache_conv[l]), lambda_init, wts)
        for lst, val in zip(outs, (kp, vp, cp, kn, vn, cn)):
            lst.append(val)
    return (yp, ys) + tuple(jnp.stack(lst) for lst in outs)
```

```python
import functools
import math

import jax
import jax.numpy as jnp
from jax import lax
from jax.experimental import pallas as pl
from jax.experimental.pallas import tpu as pltpu
from jax.experimental.pallas import tpu_sc as plsc

D_MODEL = 1024
CHUNK = 64
N_HEADS = 8
HEAD_DIM = 64
V_DIM = 2 * HEAD_DIM
ATTN_SCALE = HEAD_DIM ** -0.5
CONV_K = 31
PEER_HEADS = 8
N_KEYS = 128
PEER_TOPK = 16
EPS = 1e-6
NEG_INF = -1e30
HALF = D_MODEL // 2

LANES = 128
HIST_ROWS = 32
VMEM_LIMIT = 48 * 1024 * 1024

_BF16 = jnp.bfloat16
_F32 = jnp.float32


def _resident(shape):
    nd = len(shape)
    return pl.BlockSpec(shape, lambda *_: (0,) * nd, pipeline_mode=pl.Buffered(1))


def _params(*sem):
    return pltpu.CompilerParams(dimension_semantics=sem, vmem_limit_bytes=VMEM_LIMIT)


def _inproj_kernel(x_ref, gmix_ref, w_ref, bg_ref, gq_ref, gk_ref,
                   qs_ref, kf_ref, kb_ref, vf_ref, vb_ref, u_ref, gt_ref, *, v_transposed):
    x = x_ref[...]
    ms = jnp.mean(x * x, axis=-1, keepdims=True)
    xn = (x * lax.rsqrt(ms + EPS) * gmix_ref[...]).astype(_BF16)

    def proj(c):
        return jnp.dot(xn, w_ref[:, c * D_MODEL:(c + 1) * D_MODEL], preferred_element_type=_F32)

    lo = lax.broadcasted_iota(jnp.int32, (1, LANES), 1) < HEAD_DIM

    def head_norm(zh, g):
        sq = zh * zh
        ss_lo = jnp.sum(jnp.where(lo, sq, 0.0), axis=-1, keepdims=True)
        ss_hi = jnp.sum(jnp.where(lo, 0.0, sq), axis=-1, keepdims=True)
        r = jnp.where(lo, lax.rsqrt(ss_lo / HEAD_DIM + EPS), lax.rsqrt(ss_hi / HEAD_DIM + EPS))
        return zh * r * g

    zq = proj(0)
    for h in range(N_HEADS):
        sl = slice(h * LANES, (h + 1) * LANES)
        qn = head_norm(zq[:, sl], gq_ref[...]) * ATTN_SCALE
        qs_ref[0, :, sl] = jnp.where(lo, qn, 0.0).astype(_BF16)
        qs_ref[1, :, sl] = jnp.where(lo, 0.0, qn).astype(_BF16)
    zk = proj(1)
    for h in range(N_HEADS):
        sl = slice(h * LANES, (h + 1) * LANES)
        kn = head_norm(zk[:, sl], gk_ref[...])
        kf_ref[:, sl] = kn
        kb_ref[:, sl] = kn.astype(_BF16)
    zv = proj(2)
    vf_ref[...] = zv
    if v_transposed:
        vb_ref[0] = zv.T.astype(_BF16)
    else:
        vb_ref[...] = zv.astype(_BF16)
    u_ref[...] = proj(3) * jax.nn.sigmoid(proj(4))
    gt_ref[:, :D_MODEL] = jax.nn.sigmoid(proj(5) + bg_ref[:, :D_MODEL])
    gt_ref[:, D_MODEL:] = jax.nn.sigmoid(proj(6) + bg_ref[:, D_MODEL:])


def _inproj(x, row0, n, g_mix, w_in_bf, b_gate, gq, gk, tm, v_transposed):
    assert row0 % tm == 0 and n % tm == 0, (row0, n, tm)
    row = lambda w: pl.BlockSpec((tm, w), lambda i: (i, 0))
    x_spec = pl.BlockSpec((tm, D_MODEL), lambda i: (i + row0 // tm, 0))
    if v_transposed:
        vb_spec = pl.BlockSpec((1, D_MODEL, tm), lambda i: (i, 0, 0))
        vb_shape = jax.ShapeDtypeStruct((n // tm, D_MODEL, tm), _BF16)
    else:
        vb_spec, vb_shape = row(D_MODEL), jax.ShapeDtypeStruct((n, D_MODEL), _BF16)
    return pl.pallas_call(
        functools.partial(_inproj_kernel, v_transposed=v_transposed),
        grid=(n // tm,),
        in_specs=[x_spec, _resident((1, D_MODEL)), _resident(w_in_bf.shape),
                  _resident((1, 2 * D_MODEL)), _resident((1, LANES)), _resident((1, LANES))],
        out_specs=[pl.BlockSpec((2, tm, D_MODEL), lambda i: (0, i, 0)),
                   row(D_MODEL), row(D_MODEL), row(D_MODEL), vb_spec, row(D_MODEL),
                   row(2 * D_MODEL)],
        out_shape=[jax.ShapeDtypeStruct((2, n, D_MODEL), _BF16),
                   jax.ShapeDtypeStruct((n, D_MODEL), _F32),
                   jax.ShapeDtypeStruct((n, D_MODEL), _BF16),
                   jax.ShapeDtypeStruct((n, D_MODEL), _F32),
                   vb_shape,
                   jax.ShapeDtypeStruct((n, D_MODEL), _F32),
                   jax.ShapeDtypeStruct((n, 2 * D_MODEL), _F32)],
        compiler_params=_params("parallel"),
        name="inproj",
    )(x, g_mix, w_in_bf, b_gate, gq, gk)


def _lambda(lam_ref, lambda_init):
    lam = lam_ref[...]
    a = jnp.sum(lam[0:1] * lam[1:2], axis=-1, keepdims=True)
    b = jnp.sum(lam[2:3] * lam[3:4], axis=-1, keepdims=True)
    return jnp.exp(a) - jnp.exp(b) + lambda_init


def _attn_finish(acc, l, tq, lam, gsub_ref, lambda_init):
    o = acc / l
    o = o[:tq] - lam * o[tq:]
    o = o * lax.rsqrt(jnp.mean(o * o, axis=-1, keepdims=True) + EPS) * gsub_ref[...]
    return (o * (1.0 - lambda_init)).astype(_BF16)


def _qk(q, k):
    return lax.dot_general(q, k, (((1,), (1,)), ((), ())), preferred_element_type=_F32)


ATTN_HEADS_PER_STEP = 2
ATTN_Q_TILE = 1024


def _prompt_attn_kernel(q_ref, k_ref, vt_ref, lam_ref, gsub_ref, o_ref, *, tq, tk, lambda_init):
    qi = pl.program_id(2)
    heads = range(ATTN_HEADS_PER_STEP)
    hs = lambda h: slice(h * LANES, (h + 1) * LANES)
    q = [jnp.concatenate([q_ref[0, :, hs(h)], q_ref[1, :, hs(h)]], axis=0) for h in heads]

    def step(j, carry, masked):
        off = pl.multiple_of(j * tk, tk)
        out = []
        for h in heads:
            m, l, acc = carry[h]
            s = _qk(k_ref[pl.ds(off, tk), hs(h)], q[h])
            if masked:
                kc = (off + lax.broadcasted_iota(jnp.int32, s.shape, 0)) // CHUNK
                qc = (qi * tq + lax.broadcasted_iota(jnp.int32, s.shape, 1) % tq) // CHUNK
                s = jnp.where(kc <= qc, s, NEG_INF)
            m_new = jnp.maximum(m, jnp.max(s, axis=0, keepdims=True))
            alpha = jnp.exp(m - m_new)
            p = jnp.exp(s - m_new)
            l = alpha * l + jnp.sum(p, axis=0, keepdims=True)
            acc = alpha * acc + jnp.dot(vt_ref[j, hs(h), :], p.astype(_BF16), preferred_element_type=_F32)
            out.append((m_new, l, acc))
        return tuple(out)

    init = tuple((jnp.full((1, 2 * tq), NEG_INF, _F32), jnp.zeros((1, 2 * tq), _F32),
                  jnp.zeros((V_DIM, 2 * tq), _F32)) for _ in heads)
    ratio = tq // tk
    carry = lax.fori_loop(0, qi * ratio, lambda j, c: step(j, c, False), init)
    for d in range(ratio):
        carry = step(qi * ratio + d, carry, True)
    lam = _lambda(lam_ref, lambda_init)
    for h in heads:
        _, l, acc = carry[h]
        o = acc / l
        o = o[:, :tq] - lam * o[:, tq:]
        o = o * lax.rsqrt(jnp.mean(o * o, axis=0, keepdims=True) + EPS) * gsub_ref[...]
        o_ref[:, hs(h)] = (o * (1.0 - lambda_init)).T.astype(_BF16)


def _prompt_attention(qs, kb, vt, lam4, gsub_col, batch, seq, tk, lambda_init):
    n = batch * seq
    tq = ATTN_Q_TILE if seq % ATTN_Q_TILE == 0 and ATTN_Q_TILE % tk == 0 else tk
    nq = seq // tq
    hw = ATTN_HEADS_PER_STEP * LANES
    return pl.pallas_call(
        functools.partial(_prompt_attn_kernel, tq=tq, tk=tk, lambda_init=lambda_init),
        grid=(batch, N_HEADS // ATTN_HEADS_PER_STEP, nq),
        in_specs=[pl.BlockSpec((2, tq, hw), lambda b, h, i: (0, b * nq + i, h)),
                  pl.BlockSpec((seq, hw), lambda b, h, i: (b, h)),
                  pl.BlockSpec((seq // tk, hw, tk), lambda b, h, i: (b, h, 0)),
                  _resident((4, HEAD_DIM)), _resident((V_DIM, 1))],
        out_specs=pl.BlockSpec((tq, hw), lambda b, h, i: (b * nq + i, h)),
        out_shape=jax.ShapeDtypeStruct((n, D_MODEL), _BF16),
        compiler_params=_params("parallel", "parallel", "arbitrary"),
        name="prompt_attn",
    )(qs, kb, vt, lam4, gsub_col)


def _decode_attn_kernel(q_ref, kn_ref, vn_ref, kc_ref, vc_ref, lam_ref, gsub_ref, o_ref, *, tq, lambda_init):
    q = jnp.concatenate([q_ref[0], q_ref[1]], axis=0)
    s_c = _qk(q, kc_ref[0].astype(_BF16))
    s_n = _qk(q, kn_ref[...])
    m = jnp.maximum(jnp.max(s_c, axis=-1, keepdims=True), jnp.max(s_n, axis=-1, keepdims=True))
    p_c = jnp.exp(s_c - m)
    p_n = jnp.exp(s_n - m)
    l = jnp.sum(p_c, axis=-1, keepdims=True) + jnp.sum(p_n, axis=-1, keepdims=True)
    acc = (jnp.dot(p_c.astype(_BF16), vc_ref[0].astype(_BF16), preferred_element_type=_F32)
           + jnp.dot(p_n.astype(_BF16), vn_ref[...], preferred_element_type=_F32))
    o_ref[...] = _attn_finish(acc, l, tq, _lambda(lam_ref, lambda_init), gsub_ref, lambda_init)


def _decode_attention(qs, kb, vb, cache_k, cache_v, lam4, gsub, batch, seq, lambda_init):
    past = cache_k.shape[1]
    return pl.pallas_call(
        functools.partial(_decode_attn_kernel, tq=seq, lambda_init=lambda_init),
        grid=(batch, N_HEADS),
        in_specs=[pl.BlockSpec((2, seq, LANES), lambda b, h: (0, b, h)),
                  pl.BlockSpec((seq, LANES), lambda b, h: (b, h)),
                  pl.BlockSpec((seq, LANES), lambda b, h: (b, h)),
                  pl.BlockSpec((1, past, LANES), lambda b, h: (b, 0, h)),
                  pl.BlockSpec((1, past, LANES), lambda b, h: (b, 0, h)),
                  _resident((4, HEAD_DIM)), _resident((1, V_DIM))],
        out_specs=pl.BlockSpec((seq, LANES), lambda b, h: (b, h)),
        out_shape=jax.ShapeDtypeStruct((batch * seq, D_MODEL), _BF16),
        compiler_params=_params("parallel", "parallel"),
        name="decode_attn",
    )(qs, kb, vb, cache_k, cache_v, lam4, gsub)


CONV_ROWS = 32


def _conv_kernel(u_ref, hist_ref, w_ref, b_ref, g_ref, beta_ref, y_ref, win_ref, *, tt):
    @pl.when(pl.program_id(1) == 0)
    def _():
        win_ref[0:HIST_ROWS, :] = hist_ref[0]

    win_ref[HIST_ROWS:HIST_ROWS + tt, :] = u_ref[0]
    first = HIST_ROWS - (CONV_K - 1)
    rows = min(CONV_ROWS, tt)
    for r0 in range(0, tt, rows):
        acc = jnp.broadcast_to(b_ref[...], (rows, D_MODEL))
        for j in range(CONV_K):
            acc = acc + w_ref[j:j + 1, :] * win_ref[first + r0 + j:first + r0 + j + rows, :]
        mu = jnp.mean(acc, axis=-1, keepdims=True)
        xc = acc - mu
        y = xc * lax.rsqrt(jnp.mean(xc * xc, axis=-1, keepdims=True) + EPS)
        y = y * g_ref[...] + beta_ref[...]
        y_ref[0, r0:r0 + rows, :] = (y * jax.nn.sigmoid(y)).astype(_BF16)
    win_ref[0:HIST_ROWS, :] = win_ref[tt:tt + HIST_ROWS, :]


def _conv_branch(u, hist, w_dw, b_dw, g_cln, b_cln, tt):
    batch, seq, _ = u.shape
    return pl.pallas_call(
        functools.partial(_conv_kernel, tt=tt),
        grid=(batch, seq // tt),
        in_specs=[pl.BlockSpec((1, tt, D_MODEL), lambda b, i: (b, i, 0)),
                  pl.BlockSpec((1, HIST_ROWS, D_MODEL), lambda b, i: (b, 0, 0)),
                  _resident((HIST_ROWS, D_MODEL)), _resident((1, D_MODEL)),
                  _resident((1, D_MODEL)), _resident((1, D_MODEL))],
        out_specs=pl.BlockSpec((1, tt, D_MODEL), lambda b, i: (b, i, 0)),
        out_shape=jax.ShapeDtypeStruct((batch, seq, D_MODEL), _BF16),
        scratch_shapes=[pltpu.VMEM((HIST_ROWS + max(tt, HIST_ROWS), D_MODEL), _F32)],
        compiler_params=_params("parallel", "arbitrary"),
        name="conv_branch",
    )(u, hist, w_dw, b_dw, g_cln, b_cln)


_KEY_SENTINEL = N_KEYS * N_KEYS


def _topk_rows(s, keys, k):
    vals, sels = [], []
    for r in range(k):
        m = jnp.max(s, axis=0, keepdims=True)
        sel = jnp.min(jnp.where(s == m, keys, _KEY_SENTINEL), axis=0, keepdims=True)
        vals.append(m)
        sels.append(sel)
        if r + 1 < k:
            s = jnp.where(keys == sel, -jnp.inf, s)
    return jnp.concatenate(vals, axis=0), jnp.concatenate(sels, axis=0)


def _route_tile(hn_bf, wpq_ref, sk_ref, idx_ref, g_ref, tm):
    q = jnp.dot(hn_bf, wpq_ref[...], preferred_element_type=_F32).astype(_BF16)
    key_iota = lax.broadcasted_iota(jnp.int32, (N_KEYS, LANES), 0)
    for lt in range(tm // LANES):
        rows = slice(lt * LANES, (lt + 1) * LANES)
        idx_parts, g_parts = [], []
        for h in range(PEER_HEADS):
            top = []
            for p in range(2):
                hp = 2 * h + p
                st = _qk(sk_ref[hp], q[rows, hp * LANES:(hp + 1) * LANES])
                top.append(_topk_rows(st, key_iota, PEER_TOPK))
            (ts0, ti0), (ts1, ti1) = top
            half = PEER_TOPK // 2
            cand_s = jnp.concatenate([ts0[0:1] + ts1]
                                     + [ts0[i:i + 1] + ts1[:half] for i in range(1, half)]
                                     + [ts0[half:] + ts1[0:1]], axis=0)
            cand_i = jnp.concatenate([ti0[0:1] * N_KEYS + ti1]
                                     + [ti0[i:i + 1] * N_KEYS + ti1[:half] for i in range(1, half)]
                                     + [ti0[half:] * N_KEYS + ti1[0:1]], axis=0)
            best_s, best_i = _topk_rows(cand_s, cand_i, PEER_TOPK)
            e = jnp.exp(best_s - best_s[0:1])
            g_parts.append(e / jnp.sum(e, axis=0, keepdims=True))
            idx_parts.append(best_i)
        idx_ref[rows, :] = jnp.concatenate(idx_parts, axis=0).T
        g_ref[rows, :] = jnp.concatenate(g_parts, axis=0).T


def _merge_route_kernel(o_ref, c_ref, gt_ref, x_ref, wa_ref, wc_ref, wo_ref, gffn_ref, wpq_ref, sk_ref,
                        h_ref, xp_ref, idx_ref, g_ref, *, tm):
    attn = jnp.dot(o_ref[...], wa_ref[...], preferred_element_type=_F32)
    conv = jnp.dot(c_ref[...], wc_ref[...], preferred_element_type=_F32)
    mix = gt_ref[:, :D_MODEL] * attn + gt_ref[:, D_MODEL:] * conv
    h = x_ref[...] + jnp.dot(mix.astype(_BF16), wo_ref[...], preferred_element_type=_F32)
    h_ref[...] = h
    hn = h * lax.rsqrt(jnp.mean(h * h, axis=-1, keepdims=True) + EPS) * gffn_ref[...]
    hn_bf = hn.astype(_BF16)
    bits = pltpu.bitcast(hn_bf.astype(_F32), jnp.uint32)
    xp_ref[...] = (bits[:, :HALF] >> 16) | (bits[:, HALF:] & jnp.uint32(0xFFFF0000))
    _route_tile(hn_bf, wpq_ref, sk_ref, idx_ref, g_ref, tm)


def _merge_route(o, c, gt, x, row0, wa, wc, wo, g_ffn, w_pq_bf, sk_bf, tm):
    n = o.shape[0]
    width = PEER_HEADS * PEER_TOPK
    row = lambda w: pl.BlockSpec((tm, w), lambda i: (i, 0))
    x_spec = pl.BlockSpec((tm, D_MODEL), lambda i: (i + row0 // tm, 0))
    sq = (D_MODEL, D_MODEL)
    return pl.pallas_call(
        functools.partial(_merge_route_kernel, tm=tm),
        grid=(n // tm,),
        in_specs=[row(D_MODEL), row(D_MODEL), row(2 * D_MODEL), x_spec,
                  _resident(sq), _resident(sq), _resident(sq), _resident((1, D_MODEL)),
                  _resident(w_pq_bf.shape), _resident(sk_bf.shape)],
        out_specs=[row(D_MODEL), row(HALF), row(width), row(width)],
        out_shape=[jax.ShapeDtypeStruct((n, D_MODEL), _F32), jax.ShapeDtypeStruct((n, HALF), jnp.uint32),
                   jax.ShapeDtypeStruct((n, width), jnp.int32), jax.ShapeDtypeStruct((n, width), _F32)],
        compiler_params=_params("parallel"),
        name="merge_route",
    )(o, c, gt, x, wa, wc, wo, g_ffn, w_pq_bf, sk_bf)


SC_CORES = 2
SC_SUBCORES = 16
SC_LANES = 16
SC_TOKENS = 8
SC_RING = 4
SC_BF16_TERMS = 4
GELU_C = math.sqrt(2.0 / math.pi)


def _pack_experts(expert_u, expert_v):
    def words(t):
        b = lax.bitcast_convert_type(t.astype(_BF16), jnp.uint16).astype(jnp.uint32)
        return b[:, :HALF] | (b[:, HALF:] << 16)
    return jnp.concatenate([words(expert_u), words(expert_v)], axis=1)


def _retrieve_body(idx_hbm, g_hbm, xp_hbm, h_hbm, tab_hbm, y_hbm,
                   idx_v, g_v, xp_v, o_v, rows_v, tr_v, in_sems, out_sems, row_sems, *, nblk, tb):
    lanes = SC_LANES
    wid = lax.axis_index("s") * SC_CORES + lax.axis_index("c")
    lane = lax.iota(jnp.int32, lanes)
    chunks = tb * PEER_HEADS
    mask_hi = jnp.uint32(0xFFFF0000)

    def halves(words):
        return plsc.bitcast(words << 16, _F32), plsc.bitcast(words & mask_hi, _F32)

    def bf16_dot(a, b):
        return (a[0] * b[0] + a[1] * b[1]) + (a[2] * b[2] + a[3] * b[3])

    def stage_in(b, bs):
        t0 = (wid * nblk + b) * tb
        return [pltpu.make_async_copy(idx_hbm.at[pl.ds(t0 * PEER_HEADS, chunks)], idx_v.at[bs], in_sems.at[bs]),
                pltpu.make_async_copy(g_hbm.at[pl.ds(t0 * PEER_HEADS, chunks)], g_v.at[bs], in_sems.at[bs]),
                pltpu.make_async_copy(xp_hbm.at[pl.ds(t0, tb)], xp_v.at[bs], in_sems.at[bs]),
                pltpu.make_async_copy(h_hbm.at[pl.ds(t0, tb)], o_v.at[bs], in_sems.at[bs])]

    def stage_out(b, bs):
        t0 = (wid * nblk + b) * tb
        return pltpu.make_async_copy(o_v.at[bs], y_hbm.at[pl.ds(t0, tb)], out_sems.at[bs])

    def gather(bs, q, slot):
        return pltpu.make_async_copy(tab_hbm.at[idx_v.at[bs, q]], rows_v.at[slot], row_sems.at[slot])

    def head_chunk(bs, q, slot):
        tok = q // PEER_HEADS

        def dot_step(kc, accs):
            offs = [pl.multiple_of(kc * (SC_BF16_TERMS * lanes), SC_BF16_TERMS * lanes) + i * lanes
                    for i in range(SC_BF16_TERMS)]
            xs = [plsc.bitcast(xp_v[bs, tok, pl.ds(o, lanes)], _BF16) for o in offs]
            out = []
            for r in range(PEER_TOPK):
                us = [plsc.bitcast(rows_v[slot, r, pl.ds(o, lanes)], _BF16) for o in offs]
                lo, hi = halves(plsc.bitcast(bf16_dot(us, xs), jnp.uint32))
                out.append(accs[r] + (lo + hi))
            return tuple(out)

        zeros = tuple(jnp.zeros((lanes,), _F32) for _ in range(PEER_TOPK))
        accs = plsc.parallel_loop(0, HALF // (SC_BF16_TERMS * lanes), carry=zeros)(dot_step)
        for r in range(PEER_TOPK):
            tr_v[r, :] = accs[r]
        d = jnp.zeros((lanes,), _F32)
        for c in range(lanes):
            d = d + plsc.load_gather(tr_v, [lane, jnp.full((lanes,), c, jnp.int32)])
        z = GELU_C * (d + 0.044715 * d * d * d)
        wgt = g_v[bs, q, :] * (0.5 * d * (2.0 - 2.0 / (jnp.exp(2.0 * z) + 1.0)))
        pairs = []
        for r in range(PEER_TOPK):
            sp = wgt.at[jnp.full((lanes,), r, jnp.int32)].get(mode="promise_in_bounds")
            pairs.append(plsc.pack(sp, sp, format=plsc.PackFormat.INTERLEAVED))

        def acc_step(kc):
            off = pl.multiple_of(kc * lanes, lanes)
            a_lo = jnp.zeros((lanes,), _F32)
            a_hi = jnp.zeros((lanes,), _F32)
            for r in range(0, PEER_TOPK, SC_BF16_TERMS):
                vs = [plsc.bitcast(rows_v[slot, r + i, pl.ds(off + HALF, lanes)], _BF16)
                      for i in range(SC_BF16_TERMS)]
                lo, hi = halves(plsc.bitcast(bf16_dot(vs, pairs[r:r + SC_BF16_TERMS]), jnp.uint32))
                a_lo = a_lo + lo
                a_hi = a_hi + hi
            plsc.addupdate(o_v.at[bs, tok, pl.ds(off, lanes)], a_lo)
            plsc.addupdate(o_v.at[bs, tok, pl.ds(off + HALF, lanes)], a_hi)

        plsc.parallel_loop(0, HALF // lanes)(acc_step)

    for c in stage_in(0, 0):
        c.start()

    def block(b, _):
        bs = b % 2
        for c in stage_in(b, bs):
            c.wait()

        @pl.when(b + 1 < nblk)
        def _():
            @pl.when(b >= 1)
            def _():
                stage_out(b - 1, 1 - bs).wait()
            for c in stage_in(b + 1, 1 - bs):
                c.start()

        for s in range(SC_RING - 1):
            gather(bs, s, s).start()

        def chunk(q, _):
            nxt = q + SC_RING - 1

            @pl.when(nxt < chunks)
            def _():
                gather(bs, nxt, nxt % SC_RING).start()

            gather(bs, q, q % SC_RING).wait()
            head_chunk(bs, q, q % SC_RING)
            return 0

        lax.fori_loop(0, chunks, chunk, 0)
        stage_out(b, bs).start()
        return 0

    lax.fori_loop(0, nblk, block, 0)
    if nblk >= 2:
        stage_out(nblk - 2, (nblk - 2) % 2).wait()
    stage_out(nblk - 1, (nblk - 1) % 2).wait()


def _retrieve(idx, table, g, xp, h):
    n = h.shape[0]
    workers = SC_CORES * SC_SUBCORES
    tb = SC_TOKENS
    assert n % (workers * tb) == 0 and tb * PEER_HEADS >= SC_RING, n
    mesh = plsc.VectorSubcoreMesh(core_axis_name="c", subcore_axis_name="s")
    chunks = tb * PEER_HEADS
    call = pl.kernel(
        functools.partial(_retrieve_body, nblk=n // (workers * tb), tb=tb),
        out_type=jax.ShapeDtypeStruct((n, D_MODEL), _F32),
        mesh=mesh,
        scratch_types=[pltpu.VMEM((2, chunks, PEER_TOPK), jnp.int32),
                       pltpu.VMEM((2, chunks, PEER_TOPK), _F32),
                       pltpu.VMEM((2, tb, HALF), jnp.uint32),
                       pltpu.VMEM((2, tb, D_MODEL), _F32),
                       pltpu.VMEM((SC_RING, PEER_TOPK, D_MODEL), jnp.uint32),
                       pltpu.VMEM((PEER_TOPK, SC_LANES), _F32),
                       pltpu.SemaphoreType.DMA((2,)),
                       pltpu.SemaphoreType.DMA((2,)),
                       pltpu.SemaphoreType.DMA((SC_RING,))],
        compiler_params=pltpu.CompilerParams(needs_layout_passes=False),
        name="peer_retrieve",
    )
    return call(idx.reshape(n * PEER_HEADS, PEER_TOPK), g.reshape(n * PEER_HEADS, PEER_TOPK), xp, h, table)


def _row_tile(n):
    return min(256, n)


PROMPT_GROUPS = 4


def _layer(x_prompt, x_decode, cache_k, cache_v, hist_decode, lambda_init, wts):
    pb, seq, _ = x_prompt.shape
    groups = math.gcd(PROMPT_GROUPS, pb)
    per = pb // groups
    xp_flat = x_prompt.reshape(pb * seq, D_MODEL)
    zero_hist = jnp.zeros((per, HIST_ROWS, D_MODEL), x_prompt.dtype)
    prompt_outs = [_layer_group(xp_flat, gi * per * seq, per, seq, None, None, zero_hist, lambda_init, wts)
                   for gi in range(groups)]
    if groups > 1:
        prompt_outs = [tuple(jnp.concatenate(parts, axis=0) for parts in zip(*prompt_outs))]
    decode_outs = _layer_group(x_decode.reshape(-1, D_MODEL), 0, x_decode.shape[0], x_decode.shape[1],
                               cache_k, cache_v, hist_decode, lambda_init, wts)
    return prompt_outs[0], decode_outs


def _layer_group(xf, row0, batch, seq, cache_k, cache_v, hist, lambda_init, wts):
    g_mix = wts["g_mix"]
    n = batch * seq
    tm = _row_tile(n)
    prompt = cache_k is None
    qs, kf, kb, vf, vb, u, gt = _inproj(xf, row0, n, g_mix, wts["w_in"], wts["b_gate"], wts["gq"], wts["gk"], tm,
                                        v_transposed=prompt)
    if prompt:
        assert seq % tm == 0, (seq, tm)
        o = _prompt_attention(qs, kb, vb, wts["lam4"], wts["g_sub"].reshape(V_DIM, 1), batch, seq, tm,
                              lambda_init)
    else:
        o = _decode_attention(qs, kb, vb, cache_k, cache_v, wts["lam4"], wts["g_sub"], batch, seq, lambda_init)
    u3 = u.reshape(batch, seq, D_MODEL)
    c = _conv_branch(u3, hist, wts["w_dw"], wts["b_dw"], wts["g_cln"], wts["b_cln"], min(256, seq))
    h, xp, idx, g = _merge_route(o, c.reshape(n, D_MODEL), gt, xf, row0, wts["w_attn_out"], wts["w_conv_out"],
                                 wts["w_out"], wts["g_ffn"], wts["w_pq"], wts["sub_keys"], tm)
    y = _retrieve(idx, wts["table"], g, xp, h)
    keep = CONV_K - 1
    if seq >= keep:
        conv_state = u3[:, seq - keep:]
    else:
        conv_state = jnp.concatenate([hist[:, HIST_ROWS - keep + seq:], u3], axis=1)
    return (y.reshape(batch, seq, D_MODEL), kf.reshape(batch, seq, N_HEADS, 2, HEAD_DIM),
            vf.reshape(batch, seq, N_HEADS, V_DIM), conv_state)


def _layer_weights(l, g_mix, w_in, b_gate, g_qn, g_kn, lam_q1, lam_k1, lam_q2, lam_k2, g_sub,
                   w_attn_out, w_dw, b_dw, g_cln, b_cln, w_conv_out, w_out, g_ffn, w_pq, sub_keys,
                   expert_u, expert_v):
    row = lambda a: a[l].reshape(1, -1)
    return dict(
        g_mix=row(g_mix), w_in=w_in[l].astype(_BF16), b_gate=row(b_gate),
        gq=jnp.tile(g_qn[l], 2).reshape(1, LANES), gk=jnp.tile(g_kn[l], 2).reshape(1, LANES),
        lam4=jnp.stack([lam_q1[l], lam_k1[l], lam_q2[l], lam_k2[l]]), g_sub=row(g_sub),
        w_attn_out=w_attn_out[l].astype(_BF16),
        w_dw=jnp.pad(w_dw[l].reshape(CONV_K, D_MODEL), ((0, HIST_ROWS - CONV_K), (0, 0))),
        b_dw=row(b_dw), g_cln=row(g_cln), b_cln=row(b_cln),
        w_conv_out=w_conv_out[l].astype(_BF16), w_out=w_out[l].astype(_BF16), g_ffn=row(g_ffn),
        w_pq=w_pq[l].astype(_BF16),
        sub_keys=sub_keys[l].reshape(2 * PEER_HEADS, N_KEYS, -1).astype(_BF16),
        table=_pack_experts(expert_u[l], expert_v[l]),
    )


def kernel(x_prompt, x_sample, cache_k, cache_v, cache_conv, g_mix, w_in, b_gate, g_qn, g_kn, lam_q1, lam_k1, lam_q2, lam_k2, g_sub, w_attn_out, w_dw, b_dw, g_cln, b_cln, w_conv_out, w_out, g_ffn, w_pq, sub_keys, expert_u, expert_v):
    depth = w_in.shape[0]
    yp, ys = x_prompt, x_sample
    pad_hist = lambda hst: jnp.pad(hst, ((0, 0), (HIST_ROWS - (CONV_K - 1), 0), (0, 0)))
    outs = [[] for _ in range(6)]
    for l in range(depth):
        lambda_init = 0.8 - 0.6 * math.exp(-0.3 * l)
        wts = _layer_weights(l, g_mix, w_in, b_gate, g_qn, g_kn, lam_q1, lam_k1, lam_q2, lam_k2, g_sub,
                             w_attn_out, w_dw, b_dw, g_cln, b_cln, w_conv_out, w_out, g_ffn, w_pq,
                             sub_keys, expert_u, expert_v)
        db, past = cache_k.shape[1], cache_k.shape[2]
        (yp, kp, vp, cp), (ys, kn, vn, cn) = _layer(
            yp, ys, cache_k[l].reshape(db, past, D_MODEL), cache_v[l].reshape(db, past, D_MODEL),
            pad_hist(cache_conv[l]), lambda_init, wts)
        for lst, val in zip(outs, (kp, vp, cp, kn, vn, cn)):
            lst.append(val)
    return (yp, ys) + tuple(jnp.stack(lst) for lst in outs)
```
